```python
import math
import jax
import jax.numpy as jnp
from jax import lax
import numpy as np

D_MODEL = 2048
BATCH = 8
SEQ = 2048
DEPTH = 1
DEC_BATCH = 128
DEC_SEQ = 4
PAST_LEN = 16384
PAGE_SIZE = 128

HEAD_DIM = 64
A_WINDOW = 128
A_Q_HEADS = 16
A_KV_HEADS = 2
A_GROUP = A_Q_HEADS // A_KV_HEADS
B_PATTERNS = ((128, 1), (512, 4), (2048, 16))
B_N_GROUPS = 3
B_HEADS = 8
N_HEADS_TOTAL = A_Q_HEADS + B_N_GROUPS * B_HEADS
NUM_BUCKETS = 32
REL_MAX_DIST = 2048
BLOCK = 128
D_FF = -(-8 * D_MODEL // (3 * 256)) * 256
ALPHA = (2 * DEPTH) ** 0.25
BETA = (8 * DEPTH) ** -0.25
SCALE = HEAD_DIM ** -0.5
LN_EPS = 1e-5
A_OUT_W = A_Q_HEADS * HEAD_DIM
A_KV_W = A_KV_HEADS * HEAD_DIM
B_QKV_W = B_N_GROUPS * B_HEADS * HEAD_DIM
B_OUT_W = B_HEADS * HEAD_DIM
SPLIT_POINTS = (A_OUT_W, A_OUT_W + A_KV_W, A_OUT_W + 2 * A_KV_W, A_OUT_W + 2 * A_KV_W + B_QKV_W, A_OUT_W + 2 * A_KV_W + 2 * B_QKV_W, A_OUT_W + 2 * A_KV_W + 3 * B_QKV_W, A_OUT_W + 2 * A_KV_W + 3 * B_QKV_W + D_MODEL)
IN_W = A_OUT_W + 2 * A_KV_W + 3 * B_QKV_W + 2 * D_MODEL

kernel_name = 'hybrid_swa_sink_dilated_gated_step'


def rel_bucket(dist):
    max_exact = NUM_BUCKETS // 2
    d = jnp.maximum(dist, 0)
    ratio = jnp.maximum(d, max_exact).astype(jnp.float32) / max_exact
    large = max_exact + (jnp.log(ratio) / math.log(REL_MAX_DIST / max_exact) * (NUM_BUCKETS - max_exact)).astype(jnp.int32)
    return jnp.where(d < max_exact, d, jnp.minimum(large, NUM_BUCKETS - 1))


def rel_bias_for(rel_bias, dist, h0, nh):
    return jnp.moveaxis(rel_bias[rel_bucket(dist), h0:h0 + nh], -1, 0).astype(jnp.float32)


def softmax_weights(s, sink):
    m = jnp.max(s, axis=-1, keepdims=True)
    if sink is not None:
        m = jnp.maximum(m, sink)
    p = jnp.exp(s - m)
    l = jnp.sum(p, axis=-1, keepdims=True)
    if sink is not None:
        l = l + jnp.exp(sink - m)
    return p / l, (m + jnp.log(l))[..., 0]


def layer_norm(x, g, b):
    xf = x.astype(jnp.float32)
    mu = jnp.mean(xf, axis=-1, keepdims=True)
    var = jnp.mean(jnp.square(xf - mu), axis=-1, keepdims=True)
    return ((xf - mu) * lax.rsqrt(var + LN_EPS) * g.astype(jnp.float32) + b.astype(jnp.float32)).astype(x.dtype)


def in_proj(x, w):
    b, s = x.shape[:2]
    h = jnp.einsum('bsd,de->bse', x, w)
    qa, ka, va, qb, kb, vb, ga, gb = jnp.split(h, SPLIT_POINTS, axis=-1)
    return (qa.reshape(b, s, A_KV_HEADS, A_GROUP, HEAD_DIM),
            ka.reshape(b, s, A_KV_HEADS, HEAD_DIM),
            va.reshape(b, s, A_KV_HEADS, HEAD_DIM),
            qb.reshape(b, s, B_N_GROUPS, B_HEADS, HEAD_DIM),
            kb.reshape(b, s, B_N_GROUPS, B_HEADS, HEAD_DIM),
            vb.reshape(b, s, B_N_GROUPS, B_HEADS, HEAD_DIM),
            ga, gb)


def banded_attention(q, k, v, max_dist, dist_scale, rel_bias, h0, sink):
    n, L, kvh, g, dh = q.shape
    nb = -(-L // BLOCK)
    pad = nb * BLOCK - L
    qb = jnp.pad(q, ((0, 0), (0, pad), (0, 0), (0, 0), (0, 0))).reshape(n, nb, BLOCK, kvh, g, dh)

    def band(t):
        t = jnp.pad(t, ((0, 0), (BLOCK, pad), (0, 0), (0, 0))).reshape(n, nb + 1, BLOCK, kvh, dh)
        return jnp.concatenate([t[:, :-1], t[:, 1:]], axis=2)

    kb, vb = band(k), band(v)
    dist = jnp.arange(BLOCK)[:, None] + BLOCK - jnp.arange(2 * BLOCK)[None, :]
    key_pos = jnp.arange(nb)[:, None, None] * BLOCK + jnp.arange(2 * BLOCK)[None, None, :] - BLOCK
    valid = (dist >= 0) & (dist <= max_dist) & (key_pos >= 0)
    bias = rel_bias_for(rel_bias, dist * dist_scale, h0, kvh * g).reshape(kvh, g, BLOCK, 2 * BLOCK)
    s = jnp.einsum('nbqkgd,nbckd->nbkgqc', qb, kb, preferred_element_type=jnp.float32) * SCALE + bias
    s = jnp.where(valid[None, :, None, None], s, -jnp.inf)
    sink_b = None if sink is None else sink[None, None, :, :, None, None].astype(jnp.float32)
    p, lse = softmax_weights(s, sink_b)
    o = jnp.einsum('nbkgqc,nbckd->nbqkgd', p.astype(v.dtype), vb).reshape(n, nb * BLOCK, kvh, g, dh)[:, :L]
    lse = jnp.transpose(lse, (0, 1, 4, 2, 3)).reshape(n, nb * BLOCK, kvh, g)[:, :L]
    return o, lse


def to_residues(t, d):
    b, s = t.shape[:2]
    return jnp.moveaxis(t.reshape(b, s // d, d, *t.shape[2:]), 2, 1).reshape(b * d, s // d, *t.shape[2:])


def from_residues(t, d, b):
    n, m = t.shape[:2]
    return jnp.moveaxis(t.reshape(b, d, m, *t.shape[2:]), 1, 2).reshape(b, m * d, *t.shape[2:])


def dilated_attention_prompt(q, k, v, win, dil, rel_bias, h0):
    b = q.shape[0]
    o, lse = banded_attention(to_residues(q, dil)[:, :, :, None], to_residues(k, dil), to_residues(v, dil),
                              win // dil, dil, rel_bias, h0, None)
    return from_residues(o[:, :, :, 0], dil, b), from_residues(lse[..., 0], dil, b)


def window_attention_step(q, k, v, cache, rel_bias, sink):
    t = q.shape[1]
    lc = cache.shape[1]
    kv_all = jnp.concatenate([cache, jnp.stack([k, v], axis=2)], axis=1)
    dist = (lc + jnp.arange(t))[:, None] - jnp.arange(lc + t)[None, :]
    valid = (dist >= 0) & (dist < A_WINDOW)
    bias = rel_bias_for(rel_bias, dist, 0, A_Q_HEADS).reshape(A_KV_HEADS, A_GROUP, t, lc + t)
    s = jnp.einsum('btkgd,bckd->bkgtc', q, kv_all[:, :, 0], preferred_element_type=jnp.float32) * SCALE + bias
    s = jnp.where(valid, s, -jnp.inf)
    p, _ = softmax_weights(s, sink[None, :, :, None, None].astype(jnp.float32))
    o = jnp.einsum('bkgtc,bckd->btkgd', p.astype(v.dtype), kv_all[:, :, 1])
    return o, kv_all[:, t:]


def dilated_attention_step(q, k, v, cache, win, dil, rel_bias, h0):
    t = q.shape[1]
    lc = cache.shape[1]
    n_keys = win // dil + 1
    kv_all = jnp.concatenate([cache, jnp.stack([k, v], axis=2)], axis=1)
    idx = lc + jnp.arange(t)[:, None] - jnp.arange(n_keys)[None, :] * dil
    valid = idx >= 0
    kvg = jnp.take(kv_all, jnp.maximum(idx, 0), axis=1)
    bias = rel_bias_for(rel_bias, jnp.arange(n_keys) * dil, h0, B_HEADS)
    s = jnp.einsum('bthd,btjhd->bhtj', q, kvg[:, :, :, 0], preferred_element_type=jnp.float32) * SCALE + bias[:, None, :]
    s = jnp.where(valid, s, -jnp.inf)
    p, lse = softmax_weights(s, None)
    o = jnp.einsum('bhtj,btjhd->bthd', p.astype(v.dtype), kvg[:, :, :, 1])
    return o, jnp.moveaxis(lse, 1, 2), kv_all[:, t:]


def combine_dilations(outs, lses):
    w = jax.nn.softmax(jnp.stack(lses, axis=0), axis=0)
    o = jnp.einsum('gbsh,gbshd->bshd', w, jnp.stack(outs, axis=0).astype(jnp.float32))
    return o.reshape(o.shape[0], o.shape[1], B_OUT_W).astype(outs[0].dtype)


def merge_and_ffn(x, oa, ob, ga, gb, w_oa, w_ob, w_out, ln1_g, ln1_b, w_ffn_in, w_ffn_out, ln2_g, ln2_b):
    b, s = x.shape[:2]
    pa = oa.reshape(b, s, A_OUT_W) @ w_oa
    pb = ob @ w_ob
    mix = (jax.nn.sigmoid(ga) * pa + jax.nn.sigmoid(gb) * pb) @ w_out
    h = layer_norm(ALPHA * x + mix, ln1_g, ln1_b)
    gate, up = jnp.split(h @ w_ffn_in, 2, axis=-1)
    f = (jax.nn.silu(gate) * up) @ w_ffn_out
    return layer_norm(ALPHA * h + f, ln2_g, ln2_b)


def setup_inputs(seed: int = 0) -> dict:
    key = jax.random.key(seed)
    ks = jax.random.split(key, 20)

    def nrm(k, shape, scale=1.0):
        return jax.random.normal(k, shape, jnp.float32) * scale

    return {
        'x_prompt': nrm(ks[0], (BATCH, SEQ, D_MODEL)),
        'x_sample': nrm(ks[1], (DEC_BATCH, DEC_SEQ, D_MODEL)),
        'cache_a': nrm(ks[2], (DEPTH, DEC_BATCH, min(A_WINDOW, PAST_LEN), 2, A_KV_HEADS, HEAD_DIM)),
        'cache_b1': nrm(ks[3], (DEPTH, DEC_BATCH, min(B_PATTERNS[0][0], PAST_LEN), 2, B_HEADS, HEAD_DIM)),
        'cache_b2': nrm(ks[4], (DEPTH, DEC_BATCH, min(B_PATTERNS[1][0], PAST_LEN), 2, B_HEADS, HEAD_DIM)),
        'cache_b3': nrm(ks[5], (DEPTH, DEC_BATCH, min(B_PATTERNS[2][0], PAST_LEN), 2, B_HEADS, HEAD_DIM)),
        'rel_bias': nrm(ks[6], (NUM_BUCKETS, N_HEADS_TOTAL), 0.5),
        'w_in': nrm(ks[7], (DEPTH, D_MODEL, IN_W), D_MODEL ** -0.5),
        'a_sink': nrm(ks[8], (DEPTH, A_Q_HEADS), 0.5),
        'w_oa': nrm(ks[9], (DEPTH, A_OUT_W, D_MODEL), A_OUT_W ** -0.5),
        'w_ob': nrm(ks[10], (DEPTH, B_OUT_W, D_MODEL), B_OUT_W ** -0.5),
        'w_out': nrm(ks[11], (DEPTH, D_MODEL, D_MODEL), BETA * D_MODEL ** -0.5),
        'ln1_g': 1.0 + nrm(ks[12], (DEPTH, D_MODEL), 0.05),
        'ln1_b': nrm(ks[13], (DEPTH, D_MODEL), 0.02),
        'w_ffn_in': nrm(ks[14], (DEPTH, D_MODEL, 2 * D_FF), D_MODEL ** -0.5),
        'w_ffn_out': nrm(ks[15], (DEPTH, D_FF, D_MODEL), BETA * D_FF ** -0.5),
        'ln2_g': 1.0 + nrm(ks[16], (DEPTH, D_MODEL), 0.05),
        'ln2_b': nrm(ks[17], (DEPTH, D_MODEL), 0.02),
    }


def reference(x_prompt, x_sample, cache_a, cache_b1, cache_b2, cache_b3, rel_bias, w_in, a_sink, w_oa, w_ob,
              w_out, ln1_g, ln1_b, w_ffn_in, w_ffn_out, ln2_g, ln2_b):
    caches_b = (cache_b1, cache_b2, cache_b3)
    xp, xs = x_prompt, x_sample
    new_a_p, new_a_s = [], []
    new_b_p = [[], [], []]
    new_b_s = [[], [], []]
    for l in range(DEPTH):
        sink = a_sink[l].reshape(A_KV_HEADS, A_GROUP)
        ffn_args = (w_oa[l], w_ob[l], w_out[l], ln1_g[l], ln1_b[l], w_ffn_in[l], w_ffn_out[l], ln2_g[l], ln2_b[l])

        qa, ka, va, qb, kb, vb, ga, gb = in_proj(xp, w_in[l])
        seq = xp.shape[1]
        oa, _ = banded_attention(qa, ka, va, A_WINDOW - 1, 1, rel_bias, 0, sink)
        la = min(A_WINDOW, seq)
        new_a_p.append(jnp.stack([ka[:, seq - la:], va[:, seq - la:]], axis=2))
        outs, lses = [], []
        for g, (win, dil) in enumerate(B_PATTERNS):
            o, lse = dilated_attention_prompt(qb[:, :, g], kb[:, :, g], vb[:, :, g], win, dil, rel_bias,
                                              A_Q_HEADS + g * B_HEADS)
            outs.append(o)
            lses.append(lse)
            lb = min(win, seq)
            new_b_p[g].append(jnp.stack([kb[:, seq - lb:, g], vb[:, seq - lb:, g]], axis=2))
        ob = combine_dilations(outs, lses)
        xp_next = merge_and_ffn(xp, oa, ob, ga, gb, *ffn_args)

        qa, ka, va, qb, kb, vb, ga, gb = in_proj(xs, w_in[l])
        oa, buf_a = window_attention_step(qa, ka, va, cache_a[l], rel_bias, sink)
        new_a_s.append(buf_a)
        outs, lses = [], []
        for g, (win, dil) in enumerate(B_PATTERNS):
            o, lse, buf_b = dilated_attention_step(qb[:, :, g], kb[:, :, g], vb[:, :, g], caches_b[g][l], win, dil,
                                                   rel_bias, A_Q_HEADS + g * B_HEADS)
            outs.append(o)
            lses.append(lse)
            new_b_s[g].append(buf_b)
        ob = combine_dilations(outs, lses)
        xs_next = merge_and_ffn(xs, oa, ob, ga, gb, *ffn_args)
        xp, xs = xp_next, xs_next

    return (xp, xs,
            jnp.stack(new_a_p, axis=0), jnp.stack(new_b_p[0], axis=0), jnp.stack(new_b_p[1], axis=0), jnp.stack(new_b_p[2], axis=0),
            jnp.stack(new_a_s, axis=0), jnp.stack(new_b_s[0], axis=0), jnp.stack(new_b_s[1], axis=0), jnp.stack(new_b_s[2], axis=0))
```

```python
import functools
import math

import numpy as np
import jax
import jax.numpy as jnp
from jax import lax
from jax.experimental import pallas as pl
from jax.experimental.pallas import tpu as pltpu

F32 = jnp.float32
BF16 = jnp.bfloat16

D_MODEL = 2048
HEAD_DIM = 64
A_WINDOW = 128
A_Q_HEADS = 16
A_KV_HEADS = 2
B_PATTERNS = ((128, 1), (512, 4), (2048, 16))
B_HEADS = 8
NUM_BUCKETS = 32
REL_MAX_DIST = 2048
BLOCK = 128
D_FF = 5632
DEPTH = 1
ALPHA = (2 * DEPTH) ** 0.25
SCALE = HEAD_DIM ** -0.5
LN_EPS = 1e-5
NEG = -1e30
LANES = 128
DEC_T = 4

A_OUT_W = A_Q_HEADS * HEAD_DIM
A_KV_W = A_KV_HEADS * HEAD_DIM
B_OUT_W = B_HEADS * HEAD_DIM
C_GA, C_GB, C_QA, C_QB, C_KB, C_VB, C_KA, C_VA = 0, 2048, 4096, 5120, 6656, 8192, 9728, 9856
H_USED = 9984
HW = 10240
VMEM_LIMIT = 56 * 1024 * 1024


def _cparams(sem):
    return pltpu.CompilerParams(dimension_semantics=sem, vmem_limit_bytes=VMEM_LIMIT)


def _bucket_np(dist):
    d = np.maximum(np.asarray(dist, np.int64), 0)
    max_exact = NUM_BUCKETS // 2
    ratio = np.maximum(d, max_exact).astype(np.float32) / np.float32(max_exact)
    large = max_exact + (np.log(ratio) / np.float32(math.log(REL_MAX_DIST / max_exact))
                         * np.float32(NUM_BUCKETS - max_exact)).astype(np.int32)
    return np.where(d < max_exact, d, np.minimum(large, NUM_BUCKETS - 1)).astype(np.int32)


def _bias_table(rel_bias, dist, valid, h0, nh):
    b = jnp.take(rel_bias[:, h0:h0 + nh], jnp.asarray(_bucket_np(dist)), axis=0)
    b = jnp.moveaxis(b, -1, 0).astype(F32)
    return jnp.where(jnp.asarray(valid)[None], b, NEG)


def _band_bias(rel_bias, max_dist, dil, h0, nh):
    q = np.arange(BLOCK)[:, None]
    c = np.arange(2 * BLOCK)[None, :]
    u = q + BLOCK - c
    return _bias_table(rel_bias, u * dil, (u >= 0) & (u <= max_dist), h0, nh)


def _step_bias(rel_bias, win, dil, h0, nh, is_a):
    t = np.arange(DEC_T)[:, None]
    pos = np.arange(win)[None, :]
    tn = np.arange(LANES)[None, :]
    if is_a:
        dc = win + t - pos
        vc = (dc >= 0) & (dc < A_WINDOW)
        dn = t - tn
        vn = (dn >= 0) & (tn < DEC_T)
    else:
        dc = win + t - pos
        vc = (dc % dil == 0) & (dc // dil <= win // dil)
        dn = t - tn
        vn = (dn >= 0) & (dn % dil == 0) & (tn < DEC_T)
    bc = _bias_table(rel_bias, dc, vc, h0, nh).reshape(nh * DEC_T, win)
    bn = _bias_table(rel_bias, dn, vn, h0, nh).reshape(nh * DEC_T, LANES)
    return bc, bn


def _inproj_kernel(x_ref, w_ref, o_ref, xb_ref):
    @pl.when(pl.program_id(1) == 0)
    def _():
        xb_ref[...] = x_ref[...].astype(BF16)

    o_ref[...] = jnp.dot(xb_ref[...], w_ref[...], preferred_element_type=F32)


def _in_proj(x, w, tm, tn):
    t, k = x.shape
    n = w.shape[1]
    return pl.pallas_call(
        _inproj_kernel,
        grid=(t // tm, n // tn),
        in_specs=[pl.BlockSpec((tm, k), lambda i, j: (i, 0)),
                  pl.BlockSpec((k, tn), lambda i, j: (0, j))],
        out_specs=pl.BlockSpec((tm, tn), lambda i, j: (i, j)),
        out_shape=jax.ShapeDtypeStruct((t, n), F32),
        scratch_shapes=[pltpu.VMEM((tm, k), BF16)],
        compiler_params=_cparams(("parallel", "arbitrary")),
        name="in_proj",
    )(x, w)


def _gate_proj_kernel(oa_ref, ob_ref, ga_ref, gb_ref, woa_ref, wob_ref, o_ref):
    pa = jnp.dot(oa_ref[...], woa_ref[...], preferred_element_type=F32)
    pb = jnp.dot(ob_ref[...], wob_ref[...], preferred_element_type=F32)
    mix = jax.nn.sigmoid(ga_ref[...]) * pa + jax.nn.sigmoid(gb_ref[...]) * pb
    o_ref[...] = mix.astype(o_ref.dtype)


def _gate_proj(oa, ob, h, w_oa, w_ob, tm, tn):
    t = oa.shape[0]
    nj = D_MODEL // tn
    return pl.pallas_call(
        _gate_proj_kernel,
        grid=(t // tm, nj),
        in_specs=[pl.BlockSpec((tm, A_OUT_W), lambda i, j: (i, 0)),
                  pl.BlockSpec((tm, B_OUT_W), lambda i, j: (i, 0)),
                  pl.BlockSpec((tm, tn), lambda i, j: (i, C_GA // tn + j)),
                  pl.BlockSpec((tm, tn), lambda i, j: (i, C_GB // tn + j)),
                  pl.BlockSpec((A_OUT_W, tn), lambda i, j: (0, j)),
                  pl.BlockSpec((B_OUT_W, tn), lambda i, j: (0, j))],
        out_specs=pl.BlockSpec((tm, tn), lambda i, j: (i, j)),
        out_shape=jax.ShapeDtypeStruct((t, D_MODEL), BF16),
        compiler_params=_cparams(("parallel", "arbitrary")),
        name="gate_proj",
    )(oa, ob, h, h, w_oa, w_ob)


def _layer_norm(z, g, b):
    mu = jnp.mean(z, axis=-1, keepdims=True)
    zc = z - mu
    var = jnp.mean(zc * zc, axis=-1, keepdims=True)
    return zc * lax.rsqrt(var + LN_EPS) * g + b


def _out_ln_kernel(x_ref, m_ref, w_ref, g_ref, b_ref, o_ref):
    mix = jnp.dot(m_ref[...], w_ref[...], preferred_element_type=F32)
    o_ref[...] = _layer_norm(ALPHA * x_ref[...] + mix, g_ref[...], b_ref[...])


def _out_ln(x, mixin, w_out, g, b, tm):
    t = x.shape[0]
    return pl.pallas_call(
        _out_ln_kernel,
        grid=(t // tm,),
        in_specs=[pl.BlockSpec((tm, D_MODEL), lambda i: (i, 0)),
                  pl.BlockSpec((tm, D_MODEL), lambda i: (i, 0)),
                  pl.BlockSpec((D_MODEL, D_MODEL), lambda i: (0, 0)),
                  pl.BlockSpec((1, D_MODEL), lambda i: (0, 0)),
                  pl.BlockSpec((1, D_MODEL), lambda i: (0, 0))],
        out_specs=pl.BlockSpec((tm, D_MODEL), lambda i: (i, 0)),
        out_shape=jax.ShapeDtypeStruct((t, D_MODEL), F32),
        compiler_params=_cparams(("parallel",)),
        name="out_ln1",
    )(x, mixin, w_out, g, b)


def _ffn_in_kernel(h_ref, wg_ref, wu_ref, o_ref, hb_ref):
    @pl.when(pl.program_id(1) == 0)
    def _():
        hb_ref[...] = h_ref[...].astype(BF16)

    hb = hb_ref[...]
    gate = jnp.dot(hb, wg_ref[...], preferred_element_type=F32)
    up = jnp.dot(hb, wu_ref[...], preferred_element_type=F32)
    o_ref[...] = (gate * jax.nn.sigmoid(gate) * up).astype(o_ref.dtype)


def _ffn_in(h1, w_ffn_in, tm, tn):
    t = h1.shape[0]
    nj = D_FF // tn
    return pl.pallas_call(
        _ffn_in_kernel,
        grid=(t // tm, nj),
        in_specs=[pl.BlockSpec((tm, D_MODEL), lambda i, j: (i, 0)),
                  pl.BlockSpec((D_MODEL, tn), lambda i, j: (0, j)),
                  pl.BlockSpec((D_MODEL, tn), lambda i, j: (0, nj + j))],
        out_specs=pl.BlockSpec((tm, tn), lambda i, j: (i, j)),
        out_shape=jax.ShapeDtypeStruct((t, D_FF), BF16),
        scratch_shapes=[pltpu.VMEM((tm, D_MODEL), BF16)],
        compiler_params=_cparams(("parallel", "arbitrary")),
        name="ffn_in",
    )(h1, w_ffn_in, w_ffn_in)


def _ffn_out_kernel(u_ref, w_ref, h_ref, g_ref, b_ref, o_ref, acc_ref):
    k = pl.program_id(1)

    @pl.when(k == 0)
    def _():
        acc_ref[...] = jnp.zeros_like(acc_ref)

    acc_ref[...] += jnp.dot(u_ref[...], w_ref[...], preferred_element_type=F32)

    @pl.when(k == pl.num_programs(1) - 1)
    def _():
        o_ref[...] = _layer_norm(ALPHA * h_ref[...] + acc_ref[...], g_ref[...], b_ref[...])


def _ffn_out(u, w_ffn_out, h1, g, b, tm, tk):
    t = u.shape[0]
    return pl.pallas_call(
        _ffn_out_kernel,
        grid=(t // tm, D_FF // tk),
        in_specs=[pl.BlockSpec((tm, tk), lambda i, k: (i, k)),
                  pl.BlockSpec((tk, D_MODEL), lambda i, k: (k, 0)),
                  pl.BlockSpec((tm, D_MODEL), lambda i, k: (i, 0)),
                  pl.BlockSpec((1, D_MODEL), lambda i, k: (0, 0)),
                  pl.BlockSpec((1, D_MODEL), lambda i, k: (0, 0))],
        out_specs=pl.BlockSpec((tm, D_MODEL), lambda i, k: (i, 0)),
        out_shape=jax.ShapeDtypeStruct((t, D_MODEL), F32),
        scratch_shapes=[pltpu.VMEM((tm, D_MODEL), F32)],
        compiler_params=_cparams(("parallel", "arbitrary")),
        name="ffn_out_ln2",
    )(u, w_ffn_out, h1, g, b)


def _band_attn_kernel(*refs, n_heads, n_kv, has_sink, want_lse):
    q_ref, kc_ref, kp_ref, vc_ref, vp_ref, bias_ref = refs[:6]
    pos = 6
    sink_ref = None
    if has_sink:
        sink_ref = refs[pos]
        pos += 1
    o_ref = refs[pos]
    lse_ref = refs[pos + 1] if want_lse else None

    first = pl.program_id(2) == 0
    col = lax.broadcasted_iota(jnp.int32, (BLOCK, 2 * BLOCK), 1)
    prev_dead = jnp.logical_and(col < BLOCK, first)
    group = n_heads // n_kv
    for kv in range(n_kv):
        ksl = slice(kv * HEAD_DIM, (kv + 1) * HEAD_DIM)
        k2 = jnp.concatenate([kp_ref[:, ksl], kc_ref[:, ksl]], axis=0).astype(BF16)
        v2 = jnp.concatenate([vp_ref[:, ksl], vc_ref[:, ksl]], axis=0).astype(BF16)
        for g in range(group):
            h = kv * group + g
            hsl = slice(h * HEAD_DIM, (h + 1) * HEAD_DIM)
            qh = q_ref[:, hsl].astype(BF16)
            s = lax.dot_general(qh, k2, (((1,), (1,)), ((), ())), preferred_element_type=F32)
            s = jnp.where(prev_dead, NEG, s + bias_ref[h])
            m = jnp.max(s, axis=-1, keepdims=True)
            if has_sink:
                m = jnp.maximum(m, sink_ref[h])
            p = jnp.exp(s - m)
            l = jnp.sum(p, axis=-1, keepdims=True)
            if has_sink:
                l = l + jnp.exp(sink_ref[h] - m)
            pn = (p * (1.0 / l)).astype(BF16)
            o = jnp.dot(pn, v2, preferred_element_type=F32)
            o_ref[:, hsl] = o.astype(o_ref.dtype)
            if want_lse:
                lse_ref[:, hsl] = jnp.broadcast_to(m + jnp.log(l), (BLOCK, HEAD_DIM))


def _band_attn(h, bias, sink, *, batch, seq, dil, n_heads, n_kv, cq, ck, cv, out_dtype, want_lse):
    qw = n_heads * HEAD_DIM
    kw = n_kv * HEAD_DIM
    sd = seq // dil
    nb = sd // BLOCK
    hv = h.reshape(batch * sd, dil * HW)
    qpr, kpr = HW // qw, HW // kw

    def rows(b, r, i):
        return b * nb + i

    def prows(b, r, i):
        return b * nb + jnp.maximum(i - 1, 0)

    in_specs = [pl.BlockSpec((BLOCK, qw), lambda b, r, i: (rows(b, r, i), r * qpr + cq // qw)),
                pl.BlockSpec((BLOCK, kw), lambda b, r, i: (rows(b, r, i), r * kpr + ck // kw)),
                pl.BlockSpec((BLOCK, kw), lambda b, r, i: (prows(b, r, i), r * kpr + ck // kw)),
                pl.BlockSpec((BLOCK, kw), lambda b, r, i: (rows(b, r, i), r * kpr + cv // kw)),
                pl.BlockSpec((BLOCK, kw), lambda b, r, i: (prows(b, r, i), r * kpr + cv // kw)),
                pl.BlockSpec((n_heads, BLOCK, 2 * BLOCK), lambda b, r, i: (0, 0, 0))]
    args = [hv, hv, hv, hv, hv, bias]
    if sink is not None:
        in_specs.append(pl.BlockSpec(memory_space=pltpu.SMEM))
        args.append(sink)
    o_spec = pl.BlockSpec((BLOCK, qw), lambda b, r, i: (rows(b, r, i), r))
    o_shape = jax.ShapeDtypeStruct((batch * sd, dil * qw), out_dtype)
    out_specs, out_shape = o_spec, o_shape
    if want_lse:
        out_specs = (o_spec, o_spec)
        out_shape = (o_shape, jax.ShapeDtypeStruct((batch * sd, dil * qw), F32))
    res = pl.pallas_call(
        functools.partial(_band_attn_kernel, n_heads=n_heads, n_kv=n_kv, has_sink=sink is not None, want_lse=want_lse),
        grid=(batch, dil, nb),
        in_specs=in_specs,
        out_specs=out_specs,
        out_shape=out_shape,
        compiler_params=_cparams(("parallel", "parallel", "arbitrary")),
        name=f"band_attn_d{dil}_h{n_heads}",
    )(*args)
    if want_lse:
        return res[0].reshape(batch * seq, qw), res[1].reshape(batch * seq, qw)
    return res.reshape(batch * seq, qw)


def _combine_kernel(o1, o2, o3, l1, l2, l3, o_ref):
    a, b, c = l1[...], l2[...], l3[...]
    m = jnp.maximum(jnp.maximum(a, b), c)
    ea, eb, ec = jnp.exp(a - m), jnp.exp(b - m), jnp.exp(c - m)
    inv = 1.0 / (ea + eb + ec)
    o_ref[...] = (ea * inv * o1[...] + eb * inv * o2[...] + ec * inv * o3[...]).astype(o_ref.dtype)


def _combine(outs, lses, tm):
    t = outs[0].shape[0]
    spec = pl.BlockSpec((tm, B_OUT_W), lambda i: (i, 0))
    return pl.pallas_call(
        _combine_kernel,
        grid=(t // tm,),
        in_specs=[spec] * 6,
        out_specs=spec,
        out_shape=jax.ShapeDtypeStruct((t, B_OUT_W), BF16),
        compiler_params=_cparams(("parallel",)),
        name="combine_dilations",
    )(*outs, *lses)


def _kv_tail_kernel(k_ref, v_ref, o_ref):
    cw = k_ref.shape[1]
    o_ref[0, 0:cw, :] = k_ref[...].T
    o_ref[0, cw:2 * cw, :] = v_ref[...].T


def _kv_tail(h, *, batch, seq, win, cw, ck, cv):
    nblk = win // BLOCK
    base = (seq - win) // BLOCK
    per = seq // BLOCK
    return pl.pallas_call(
        _kv_tail_kernel,
        grid=(batch, nblk),
        in_specs=[pl.BlockSpec((BLOCK, cw), lambda b, i: (b * per + base + i, ck // cw)),
                  pl.BlockSpec((BLOCK, cw), lambda b, i: (b * per + base + i, cv // cw))],
        out_specs=pl.BlockSpec((1, 2 * cw, BLOCK), lambda b, i: (b, 0, i)),
        out_shape=jax.ShapeDtypeStruct((batch, 2 * cw, win), F32),
        compiler_params=_cparams(("parallel", "parallel")),
        name=f"kv_tail_w{win}_c{cw}",
    )(h, h)


def _step_kernel(*refs, n_heads, n_kv, win, has_sink, want_lse):
    c_ref, q_ref, kn_ref, vn_ref, bc_ref, bn_ref = refs[:6]
    pos = 6
    sink_ref = None
    if has_sink:
        sink_ref = refs[pos]
        pos += 1
    cout_ref, o_ref = refs[pos], refs[pos + 1]
    lse_ref = refs[pos + 2] if want_lse else None

    b = pl.program_id(0)
    per_tile = LANES // DEC_T
    lo = DEC_T * lax.rem(b, per_tile)
    to_front = lax.rem(LANES - lo, LANES)
    to_tail = lax.rem(2 * LANES - DEC_T - lo, LANES)
    kvw = n_kv * HEAD_DIM
    group = n_heads // n_kv
    lane = lax.broadcasted_iota(jnp.int32, (1, LANES), 1)

    keep = lane < LANES - DEC_T
    new_tail = jnp.concatenate([pltpu.roll(kn_ref[...], to_tail, 1), pltpu.roll(vn_ref[...], to_tail, 1)], axis=0)
    nlb = win // LANES
    for j in range(nlb):
        cur = pltpu.roll(c_ref[0, :, j * LANES:(j + 1) * LANES], LANES - DEC_T, 1)
        if j + 1 < nlb:
            nxt = pltpu.roll(c_ref[0, :, (j + 1) * LANES:(j + 2) * LANES], LANES - DEC_T, 1)
        else:
            nxt = new_tail
        cout_ref[0, :, j * LANES:(j + 1) * LANES] = jnp.where(keep, cur, nxt)

    @pl.when(lax.rem(b, per_tile) == 0)
    def _():
        o_ref[...] = jnp.zeros_like(o_ref)
        if want_lse:
            lse_ref[...] = jnp.zeros_like(lse_ref)

    qf = pltpu.roll(q_ref[...], to_front, 1)
    knf = pltpu.roll(kn_ref[...], to_front, 1)
    vnf = pltpu.roll(vn_ref[...], to_front, 1)
    for h in range(n_heads):
        kv = h // group
        kt = c_ref[0, kv * HEAD_DIM:(kv + 1) * HEAD_DIM, :]
        vt = c_ref[0, kvw + kv * HEAD_DIM:kvw + (kv + 1) * HEAD_DIM, :]
        knh = knf[kv * HEAD_DIM:(kv + 1) * HEAD_DIM, :]
        vnh = vnf[kv * HEAD_DIM:(kv + 1) * HEAD_DIM, :]
        qh = qf[h * HEAD_DIM:(h + 1) * HEAD_DIM, :]
        acc = jnp.zeros((HEAD_DIM, LANES), F32)
        lacc = jnp.zeros((HEAD_DIM, LANES), F32)
        for t in range(DEC_T):
            qc = qh[:, t:t + 1]
            r = h * DEC_T + t
            s = jnp.sum(kt * qc, axis=0, keepdims=True) + bc_ref[r:r + 1, :]
            sn = jnp.sum(knh * qc, axis=0, keepdims=True) + bn_ref[r:r + 1, :]
            m = jnp.maximum(jnp.max(s, axis=-1, keepdims=True), jnp.max(sn, axis=-1, keepdims=True))
            if has_sink:
                m = jnp.maximum(m, sink_ref[h])
            p = jnp.exp(s - m)
            pn = jnp.exp(sn - m)
            l = jnp.sum(p, axis=-1, keepdims=True) + jnp.sum(pn, axis=-1, keepdims=True)
            if has_sink:
                l = l + jnp.exp(sink_ref[h] - m)
            o = jnp.sum(vt * p, axis=1, keepdims=True) + jnp.sum(vnh * pn, axis=1, keepdims=True)
            o = o * (1.0 / l)
            acc = jnp.where(lane == t, o, acc)
            if want_lse:
                lacc = jnp.where(lane == t, m + jnp.log(l), lacc)
        hs = slice(h * HEAD_DIM, (h + 1) * HEAD_DIM)
        o_ref[hs, :] += pltpu.roll(acc, lo, 1)
        if want_lse:
            lse_ref[hs, :] += pltpu.roll(lacc, lo, 1)


def _step_attn(ct, ht, bias_c, bias_n, sink, *, n_heads, n_kv, win, rq, rk, rv, want_lse):
    db = ct.shape[0]
    qw, kw = n_heads * HEAD_DIM, n_kv * HEAD_DIM
    per_tile = LANES // DEC_T
    lanes = ht.shape[1]
    in_specs = [pl.BlockSpec((1, 2 * kw, win), lambda b: (b, 0, 0)),
                pl.BlockSpec((qw, LANES), lambda b: (rq // qw, b // per_tile)),
                pl.BlockSpec((kw, LANES), lambda b: (rk // kw, b // per_tile)),
                pl.BlockSpec((kw, LANES), lambda b: (rv // kw, b // per_tile)),
                pl.BlockSpec((n_heads * DEC_T, win), lambda b: (0, 0)),
                pl.BlockSpec((n_heads * DEC_T, LANES), lambda b: (0, 0))]
    args = [ct, ht, ht, ht, bias_c, bias_n]
    if sink is not None:
        in_specs.append(pl.BlockSpec(memory_space=pltpu.SMEM))
        args.append(sink)
    o_spec = pl.BlockSpec((qw, LANES), lambda b: (0, b // per_tile))
    out_specs = [pl.BlockSpec((1, 2 * kw, win), lambda b: (b, 0, 0)), o_spec]
    out_shape = [jax.ShapeDtypeStruct(ct.shape, F32), jax.ShapeDtypeStruct((qw, lanes), F32)]
    if want_lse:
        out_specs.append(o_spec)
        out_shape.append(jax.ShapeDtypeStruct((qw, lanes), F32))
    return pl.pallas_call(
        functools.partial(_step_kernel, n_heads=n_heads, n_kv=n_kv, win=win, has_sink=sink is not None, want_lse=want_lse),
        grid=(db,),
        in_specs=in_specs,
        out_specs=tuple(out_specs),
        out_shape=tuple(out_shape),
        compiler_params=_cparams(("arbitrary",)),
        name=f"step_attn_w{win}_h{n_heads}",
    )(*args)


def _prep_w_in(w):
    sp = (0, 1024, 1152, 1280, 2816, 4352, 5888, 7936, 9984)
    qa, ka, va, qb, kb, vb, ga, gb = [w[:, sp[i]:sp[i + 1]] for i in range(8)]
    pad = jnp.zeros((w.shape[0], HW - H_USED), w.dtype)
    return jnp.concatenate([ga, gb, qa * SCALE, qb * SCALE, kb, vb, ka, va, pad], axis=1).astype(BF16)


def _to_feature_major(cache):
    db, win = cache.shape[:2]
    return jnp.transpose(cache, (0, 2, 3, 4, 1)).reshape(db, -1, win)


def _from_feature_major(ct, heads):
    n, _, win = ct.shape
    return jnp.transpose(ct.reshape(n, 2, heads, HEAD_DIM, win), (0, 4, 1, 2, 3))[None]


def _tail_layers(x, h, oa, ob, w_oa, w_ob, w_out, ln1_g, ln1_b, w_ffn_in, w_ffn_out, ln2_g, ln2_b):
    t = x.shape[0]
    mixin = _gate_proj(oa, ob, h, w_oa, w_ob, min(512, t), 512)
    h1 = _out_ln(x, mixin, w_out, ln1_g, ln1_b, min(256, t))
    u = _ffn_in(h1, w_ffn_in, min(1024, t), 512)
    return _ffn_out(u, w_ffn_out, h1, ln2_g, ln2_b, min(512, t), 512)


def kernel(x_prompt, x_sample, cache_a, cache_b1, cache_b2, cache_b3, rel_bias, w_in, a_sink, w_oa, w_ob,
           w_out, ln1_g, ln1_b, w_ffn_in, w_ffn_out, ln2_g, ln2_b):
    batch, seq, _ = x_prompt.shape
    db, dt, _ = x_sample.shape
    assert dt == DEC_T and w_in.shape[0] == DEPTH
    tp, ts = batch * seq, db * dt

    w_in_b = _prep_w_in(w_in[0])
    weights = (w_oa[0].astype(BF16), w_ob[0].astype(BF16), w_out[0].astype(BF16), ln1_g, ln1_b,
               w_ffn_in[0].astype(BF16), w_ffn_out[0].astype(BF16), ln2_g, ln2_b)
    sink = a_sink[0].astype(F32)
    b_h0 = [A_Q_HEADS + g * B_HEADS for g in range(len(B_PATTERNS))]

    xp = x_prompt.reshape(tp, D_MODEL)
    hp = _in_proj(xp, w_in_b, 1024, 512)
    oa = _band_attn(hp, _band_bias(rel_bias, A_WINDOW - 1, 1, 0, A_Q_HEADS), sink, batch=batch, seq=seq, dil=1,
                    n_heads=A_Q_HEADS, n_kv=A_KV_HEADS, cq=C_QA, ck=C_KA, cv=C_VA, out_dtype=BF16, want_lse=False)
    outs, lses = [], []
    for g, (win, dil) in enumerate(B_PATTERNS):
        o, lse = _band_attn(hp, _band_bias(rel_bias, win // dil, dil, b_h0[g], B_HEADS), None, batch=batch, seq=seq,
                            dil=dil, n_heads=B_HEADS, n_kv=B_HEADS, cq=C_QB + g * B_OUT_W, ck=C_KB + g * B_OUT_W,
                            cv=C_VB + g * B_OUT_W, out_dtype=F32, want_lse=True)
        outs.append(o)
        lses.append(lse)
    ob = _combine(outs, lses, 1024)
    yp = _tail_layers(xp, hp, oa, ob, *weights).reshape(batch, seq, D_MODEL)

    new_a_p = _from_feature_major(_kv_tail(hp, batch=batch, seq=seq, win=min(A_WINDOW, seq), cw=A_KV_W, ck=C_KA, cv=C_VA),
                                  A_KV_HEADS)
    new_b_p = [_from_feature_major(_kv_tail(hp, batch=batch, seq=seq, win=min(win, seq), cw=B_OUT_W,
                                            ck=C_KB + g * B_OUT_W, cv=C_VB + g * B_OUT_W), B_HEADS)
               for g, (win, dil) in enumerate(B_PATTERNS)]

    xs = x_sample.reshape(ts, D_MODEL)
    hs = _in_proj(xs, w_in_b, ts, 512)
    lanes = -(-ts // LANES) * LANES
    ht = jnp.pad(hs[:, C_QA:H_USED].T, ((0, 0), (0, lanes - ts)))
    r_qa, r_qb, r_kb, r_vb, r_ka, r_va = (c - C_QA for c in (C_QA, C_QB, C_KB, C_VB, C_KA, C_VA))

    bc, bn = _step_bias(rel_bias, cache_a.shape[2], 1, 0, A_Q_HEADS, True)
    new_a_t, oa_t = _step_attn(_to_feature_major(cache_a[0]), ht, bc, bn, sink, n_heads=A_Q_HEADS, n_kv=A_KV_HEADS,
                               win=cache_a.shape[2], rq=r_qa, rk=r_ka, rv=r_va, want_lse=False)
    new_b_s, outs, lses = [], [], []
    for g, ((win, dil), cache) in enumerate(zip(B_PATTERNS, (cache_b1, cache_b2, cache_b3))):
        bc, bn = _step_bias(rel_bias, cache.shape[2], dil, b_h0[g], B_HEADS, False)
        new_t, o_t, lse_t = _step_attn(_to_feature_major(cache[0]), ht, bc, bn, None, n_heads=B_HEADS, n_kv=B_HEADS,
                                       win=cache.shape[2], rq=r_qb + g * B_OUT_W, rk=r_kb + g * B_OUT_W,
                                       rv=r_vb + g * B_OUT_W, want_lse=True)
        new_b_s.append(_from_feature_major(new_t, B_HEADS))
        outs.append(o_t[:, :ts].T)
        lses.append(lse_t[:, :ts].T)
    oa_s = oa_t[:, :ts].T.astype(BF16)
    ob_s = _combine(outs, lses, ts)
    ys = _tail_layers(xs, hs, oa_s, ob_s, *weights).reshape(db, dt, D_MODEL)

    return (yp, ys, new_a_p, new_b_p[0], new_b_p[1], new_b_p[2],
            _from_feature_major(new_a_t, A_KV_HEADS), new_b_s[0], new_b_s[1], new_b_s[2])
```

```python
import functools
import math

import numpy as np
import jax
import jax.numpy as jnp
from jax import lax
from jax.experimental import pallas as pl
from jax.experimental.pallas import tpu as pltpu

F32 = jnp.float32
BF16 = jnp.bfloat16

D_MODEL = 2048
HEAD_DIM = 64
A_WINDOW = 128
A_Q_HEADS = 16
A_KV_HEADS = 2
B_PATTERNS = ((128, 1), (512, 4), (2048, 16))
B_HEADS = 8
NUM_BUCKETS = 32
REL_MAX_DIST = 2048
BLOCK = 128
D_FF = 5632
DEPTH = 1
ALPHA = (2 * DEPTH) ** 0.25
SCALE = HEAD_DIM ** -0.5
LN_EPS = 1e-5
NEG = -1e30
LANES = 128
SUBLANES = 8
DEC_T = 4

A_OUT_W = A_Q_HEADS * HEAD_DIM
A_KV_W = A_KV_HEADS * HEAD_DIM
B_OUT_W = B_HEADS * HEAD_DIM
C_QA, C_KA, C_VA, C_PAD, C_QB, C_KB, C_VB, C_GA, C_GB = 0, 1024, 1152, 1280, 1536, 3072, 4608, 6144, 8192
HW = 10240
QKV_W = C_GA
VMEM_LIMIT = 56 * 1024 * 1024


def _cparams(sem):
    return pltpu.CompilerParams(dimension_semantics=sem, vmem_limit_bytes=VMEM_LIMIT)


def _bucket_np(dist):
    d = np.maximum(np.asarray(dist, np.int64), 0)
    max_exact = NUM_BUCKETS // 2
    ratio = np.maximum(d, max_exact).astype(np.float32) / np.float32(max_exact)
    large = max_exact + (np.log(ratio) / np.float32(math.log(REL_MAX_DIST / max_exact))
                         * np.float32(NUM_BUCKETS - max_exact)).astype(np.int32)
    return np.where(d < max_exact, d, np.minimum(large, NUM_BUCKETS - 1)).astype(np.int32)


def _bias_by_dist(rel_bias, dists, h0, nh):
    return jnp.take(rel_bias[:, h0:h0 + nh], jnp.asarray(_bucket_np(dists)), axis=0).T.astype(F32)


def _band_bias(rel_bias, max_dist, dil, h0, nh):
    u = BLOCK - np.arange(2 * BLOCK)
    w = jnp.where(jnp.asarray((u >= 0) & (u <= max_dist))[None],
                  _bias_by_dist(rel_bias, np.maximum(u, 0) * dil, h0, nh), NEG)
    x = jnp.concatenate([w, jnp.full((nh, 1), NEG, F32)], axis=1)
    return jnp.tile(x, (1, BLOCK))[:, :BLOCK * 2 * BLOCK].reshape(nh, BLOCK, 2 * BLOCK)


def _step_bias(rel_bias, win, dil, h0, nh, n_kv, is_a):
    neg = lambda n: jnp.full((nh, n), NEG, F32)
    rows_c = []
    if is_a:
        rev = _bias_by_dist(rel_bias, np.arange(A_WINDOW - 1, -1, -1), h0, nh)
        for t in range(DEC_T):
            rows_c.append(jnp.concatenate([neg(t + 1), rev[:, :win - t - 1]], axis=1))
    elif dil == 1:
        rev = _bias_by_dist(rel_bias, np.arange(win, 0, -1), h0, nh)
        for t in range(DEC_T):
            rows_c.append(jnp.concatenate([neg(t), rev[:, :win - t]], axis=1))
    else:
        rev = _bias_by_dist(rel_bias, np.arange(win // dil, 0, -1) * dil, h0, nh)
        for t in range(DEC_T):
            slots = [rev[:, :, None] if r == t else jnp.full((nh, win // dil, 1), NEG, F32) for r in range(dil)]
            rows_c.append(jnp.concatenate(slots, axis=2).reshape(nh, win))
    bc = jnp.stack(rows_c * 2, axis=1)

    i = np.arange(SUBLANES)[:, None]
    j = np.arange(LANES)[None, :]
    dn = i % DEC_T - j % DEC_T
    vn = (j < SUBLANES) & (i // DEC_T == j // DEC_T) & (dn >= 0)
    if not is_a:
        vn &= dn % dil == 0
    near = _bias_by_dist(rel_bias, np.arange(DEC_T), h0, nh)
    bn = jnp.where(jnp.asarray(vn)[None], near[:, np.clip(dn, 0, DEC_T - 1)], NEG)
    g8 = (nh // n_kv) * SUBLANES
    return bc.reshape(n_kv, g8, win), bn.reshape(n_kv, g8, LANES)


def _inproj_kernel(x_ref, w_ref, o_ref, xb_ref):
    @pl.when(pl.program_id(1) == 0)
    def _():
        xb_ref[...] = x_ref[...].astype(BF16)

    o_ref[...] = jnp.dot(xb_ref[...], w_ref[...], preferred_element_type=F32)


def _in_proj(x, w, tm, tn):
    t, k = x.shape
    n = w.shape[1]
    return pl.pallas_call(
        _inproj_kernel,
        grid=(t // tm, n // tn),
        in_specs=[pl.BlockSpec((tm, k), lambda i, j: (i, 0)),
                  pl.BlockSpec((k, tn), lambda i, j: (0, j))],
        out_specs=pl.BlockSpec((tm, tn), lambda i, j: (i, j)),
        out_shape=jax.ShapeDtypeStruct((t, n), F32),
        scratch_shapes=[pltpu.VMEM((tm, k), BF16)],
        compiler_params=_cparams(("parallel", "arbitrary")),
        name="in_proj",
    )(x, w)


def _gate_proj_kernel(oa_ref, ob_ref, ga_ref, gb_ref, woa_ref, wob_ref, o_ref):
    pa = jnp.dot(oa_ref[...], woa_ref[...], preferred_element_type=F32)
    pb = jnp.dot(ob_ref[...], wob_ref[...], preferred_element_type=F32)
    mix = jax.nn.sigmoid(ga_ref[...]) * pa + jax.nn.sigmoid(gb_ref[...]) * pb
    o_ref[...] = mix.astype(o_ref.dtype)


def _gate_proj(oa, ob, h, w_oa, w_ob, tm, tn):
    t = oa.shape[0]
    nj = D_MODEL // tn
    return pl.pallas_call(
        _gate_proj_kernel,
        grid=(t // tm, nj),
        in_specs=[pl.BlockSpec((tm, A_OUT_W), lambda i, j: (i, 0)),
                  pl.BlockSpec((tm, B_OUT_W), lambda i, j: (i, 0)),
                  pl.BlockSpec((tm, tn), lambda i, j: (i, C_GA // tn + j)),
                  pl.BlockSpec((tm, tn), lambda i, j: (i, C_GB // tn + j)),
                  pl.BlockSpec((A_OUT_W, tn), lambda i, j: (0, j)),
                  pl.BlockSpec((B_OUT_W, tn), lambda i, j: (0, j))],
        out_specs=pl.BlockSpec((tm, tn), lambda i, j: (i, j)),
        out_shape=jax.ShapeDtypeStruct((t, D_MODEL), BF16),
        compiler_params=_cparams(("parallel", "arbitrary")),
        name="gate_proj",
    )(oa, ob, h, h, w_oa, w_ob)


def _layer_norm(z, g, b):
    mu = jnp.mean(z, axis=-1, keepdims=True)
    zc = z - mu
    var = jnp.mean(zc * zc, axis=-1, keepdims=True)
    return zc * lax.rsqrt(var + LN_EPS) * g + b


def _out_ln_kernel(x_ref, m_ref, w_ref, g_ref, b_ref, o_ref):
    mix = jnp.dot(m_ref[...], w_ref[...], preferred_element_type=F32)
    o_ref[...] = _layer_norm(ALPHA * x_ref[...] + mix, g_ref[...], b_ref[...])


def _out_ln(x, mixin, w_out, g, b, tm):
    t = x.shape[0]
    return pl.pallas_call(
        _out_ln_kernel,
        grid=(t // tm,),
        in_specs=[pl.BlockSpec((tm, D_MODEL), lambda i: (i, 0)),
                  pl.BlockSpec((tm, D_MODEL), lambda i: (i, 0)),
                  pl.BlockSpec((D_MODEL, D_MODEL), lambda i: (0, 0)),
                  pl.BlockSpec((1, D_MODEL), lambda i: (0, 0)),
                  pl.BlockSpec((1, D_MODEL), lambda i: (0, 0))],
        out_specs=pl.BlockSpec((tm, D_MODEL), lambda i: (i, 0)),
        out_shape=jax.ShapeDtypeStruct((t, D_MODEL), F32),
        compiler_params=_cparams(("parallel",)),
        name="out_ln1",
    )(x, mixin, w_out, g, b)


def _ffn_in_kernel(h_ref, wg_ref, wu_ref, o_ref, hb_ref):
    @pl.when(pl.program_id(1) == 0)
    def _():
        hb_ref[...] = h_ref[...].astype(BF16)

    hb = hb_ref[...]
    gate = jnp.dot(hb, wg_ref[...], preferred_element_type=F32)
    up = jnp.dot(hb, wu_ref[...], preferred_element_type=F32)
    o_ref[...] = (gate * jax.nn.sigmoid(gate) * up).astype(o_ref.dtype)


def _ffn_in(h1, w_ffn_in, tm, tn):
    t = h1.shape[0]
    nj = D_FF // tn
    return pl.pallas_call(
        _ffn_in_kernel,
        grid=(t // tm, nj),
        in_specs=[pl.BlockSpec((tm, D_MODEL), lambda i, j: (i, 0)),
                  pl.BlockSpec((D_MODEL, tn), lambda i, j: (0, j)),
                  pl.BlockSpec((D_MODEL, tn), lambda i, j: (0, nj + j))],
        out_specs=pl.BlockSpec((tm, tn), lambda i, j: (i, j)),
        out_shape=jax.ShapeDtypeStruct((t, D_FF), BF16),
        scratch_shapes=[pltpu.VMEM((tm, D_MODEL), BF16)],
        compiler_params=_cparams(("parallel", "arbitrary")),
        name="ffn_in",
    )(h1, w_ffn_in, w_ffn_in)


def _ffn_out_kernel(u_ref, w_ref, h_ref, g_ref, b_ref, o_ref, acc_ref):
    k = pl.program_id(1)

    @pl.when(k == 0)
    def _():
        acc_ref[...] = jnp.zeros_like(acc_ref)

    acc_ref[...] += jnp.dot(u_ref[...], w_ref[...], preferred_element_type=F32)

    @pl.when(k == pl.num_programs(1) - 1)
    def _():
        o_ref[...] = _layer_norm(ALPHA * h_ref[...] + acc_ref[...], g_ref[...], b_ref[...])


def _ffn_out(u, w_ffn_out, h1, g, b, tm, tk):
    t = u.shape[0]
    return pl.pallas_call(
        _ffn_out_kernel,
        grid=(t // tm, D_FF // tk),
        in_specs=[pl.BlockSpec((tm, tk), lambda i, k: (i, k)),
                  pl.BlockSpec((tk, D_MODEL), lambda i, k: (k, 0)),
                  pl.BlockSpec((tm, D_MODEL), lambda i, k: (i, 0)),
                  pl.BlockSpec((1, D_MODEL), lambda i, k: (0, 0)),
                  pl.BlockSpec((1, D_MODEL), lambda i, k: (0, 0))],
        out_specs=pl.BlockSpec((tm, D_MODEL), lambda i, k: (i, 0)),
        out_shape=jax.ShapeDtypeStruct((t, D_MODEL), F32),
        scratch_shapes=[pltpu.VMEM((tm, D_MODEL), F32)],
        compiler_params=_cparams(("parallel", "arbitrary")),
        name="ffn_out_ln2",
    )(u, w_ffn_out, h1, g, b)


def _softmax_rows(s, sink):
    m = jnp.max(s, axis=-1, keepdims=True)
    if sink is not None:
        m = jnp.maximum(m, sink)
    p = jnp.exp(s - m)
    l = jnp.sum(p, axis=-1, keepdims=True)
    if sink is not None:
        l = l + jnp.exp(sink - m)
    return p * (1.0 / l), m + jnp.log(l)


def _band_attn_kernel(q_ref, kc_ref, kp_ref, vc_ref, vp_ref, bias_ref, sink_ref, o_ref, *, n_heads, n_kv):
    first = pl.program_id(1) == 0
    col = lax.broadcasted_iota(jnp.int32, (BLOCK, 2 * BLOCK), 1)
    prev_dead = jnp.logical_and(col < BLOCK, first)
    group = n_heads // n_kv
    for kv in range(n_kv):
        ksl = slice(kv * HEAD_DIM, (kv + 1) * HEAD_DIM)
        k2 = jnp.concatenate([kp_ref[:, ksl], kc_ref[:, ksl]], axis=0).astype(BF16)
        v2 = jnp.concatenate([vp_ref[:, ksl], vc_ref[:, ksl]], axis=0).astype(BF16)
        for g in range(group):
            h = kv * group + g
            hsl = slice(h * HEAD_DIM, (h + 1) * HEAD_DIM)
            qh = q_ref[:, hsl].astype(BF16)
            s = lax.dot_general(qh, k2, (((1,), (1,)), ((), ())), preferred_element_type=F32)
            s = jnp.where(prev_dead, NEG, s + bias_ref[h])
            pn, _ = _softmax_rows(s, sink_ref[h])
            o = jnp.dot(pn.astype(BF16), v2, preferred_element_type=F32)
            o_ref[:, hsl] = o.astype(o_ref.dtype)


def _band_attn_a(h, bias, sink, *, batch, seq):
    nb = seq // BLOCK
    qw, kw = A_OUT_W, A_KV_W

    def cur(b, i):
        return b * nb + i

    def prev(b, i):
        return b * nb + jnp.maximum(i - 1, 0)

    return pl.pallas_call(
        functools.partial(_band_attn_kernel, n_heads=A_Q_HEADS, n_kv=A_KV_HEADS),
        grid=(batch, nb),
        in_specs=[pl.BlockSpec((BLOCK, qw), lambda b, i: (cur(b, i), C_QA // qw)),
                  pl.BlockSpec((BLOCK, kw), lambda b, i: (cur(b, i), C_KA // kw)),
                  pl.BlockSpec((BLOCK, kw), lambda b, i: (prev(b, i), C_KA // kw)),
                  pl.BlockSpec((BLOCK, kw), lambda b, i: (cur(b, i), C_VA // kw)),
                  pl.BlockSpec((BLOCK, kw), lambda b, i: (prev(b, i), C_VA // kw)),
                  pl.BlockSpec((A_Q_HEADS, BLOCK, 2 * BLOCK), lambda b, i: (0, 0, 0)),
                  pl.BlockSpec(memory_space=pltpu.SMEM)],
        out_specs=pl.BlockSpec((BLOCK, qw), lambda b, i: (cur(b, i), 0)),
        out_shape=jax.ShapeDtypeStruct((batch * seq, qw), BF16),
        compiler_params=_cparams(("parallel", "arbitrary")),
        name="band_attn_a",
    )(h, h, h, h, h, bias, sink)


def _dil_attn_kernel(q_ref, kc_ref, kp_ref, vc_ref, vp_ref, bias_ref, o_ref, lse_ref, *, dil):
    first = pl.program_id(2) == 0
    col = lax.broadcasted_iota(jnp.int32, (BLOCK, 2 * BLOCK), 1)
    prev_dead = jnp.logical_and(col < BLOCK, first)
    lane = lax.broadcasted_iota(jnp.int32, (1, LANES), 1)
    head_lanes = [lane < HEAD_DIM, lane >= HEAD_DIM]

    def residue(r, carry):
        rows = pl.ds(r, BLOCK, stride=dil) if dil > 1 else slice(None)
        q = q_ref[rows, :]
        k2 = jnp.concatenate([kp_ref[rows, :], kc_ref[rows, :]], axis=0).astype(BF16)
        v2 = jnp.concatenate([vp_ref[rows, :], vc_ref[rows, :]], axis=0)
        o = jnp.zeros((BLOCK, LANES), F32)
        lse = jnp.zeros((BLOCK, LANES), F32)
        for hh in range(2):
            qh = jnp.where(head_lanes[hh], q, 0.0).astype(BF16)
            vh = jnp.where(head_lanes[hh], v2, 0.0).astype(BF16)
            s = lax.dot_general(qh, k2, (((1,), (1,)), ((), ())), preferred_element_type=F32)
            s = jnp.where(prev_dead, NEG, s + bias_ref[hh])
            pn, l = _softmax_rows(s, None)
            o = o + jnp.dot(pn.astype(BF16), vh, preferred_element_type=F32)
            lse = jnp.where(head_lanes[hh], l, lse)
        o_ref[rows, :] = o
        lse_ref[rows, :] = lse
        return carry

    if dil > 1:
        lax.fori_loop(0, dil, residue, 0)
    else:
        residue(0, 0)


def _dil_attn(h, bias, *, batch, seq, dil, cq, ck, cv):
    rows = BLOCK * dil
    nb = seq // rows
    pairs = B_OUT_W // LANES

    def cur(b, p, i):
        return b * nb + i

    def prev(b, p, i):
        return b * nb + jnp.maximum(i - 1, 0)

    o_spec = pl.BlockSpec((rows, LANES), lambda b, p, i: (cur(b, p, i), p))
    o_shape = jax.ShapeDtypeStruct((batch * seq, B_OUT_W), F32)
    return pl.pallas_call(
        functools.partial(_dil_attn_kernel, dil=dil),
        grid=(batch, pairs, nb),
        in_specs=[pl.BlockSpec((rows, LANES), lambda b, p, i: (cur(b, p, i), cq // LANES + p)),
                  pl.BlockSpec((rows, LANES), lambda b, p, i: (cur(b, p, i), ck // LANES + p)),
                  pl.BlockSpec((rows, LANES), lambda b, p, i: (prev(b, p, i), ck // LANES + p)),
                  pl.BlockSpec((rows, LANES), lambda b, p, i: (cur(b, p, i), cv // LANES + p)),
                  pl.BlockSpec((rows, LANES), lambda b, p, i: (prev(b, p, i), cv // LANES + p)),
                  pl.BlockSpec((2, BLOCK, 2 * BLOCK), lambda b, p, i: (p, 0, 0))],
        out_specs=(o_spec, o_spec),
        out_shape=(o_shape, o_shape),
        compiler_params=_cparams(("parallel", "parallel", "arbitrary")),
        name=f"dil_attn_d{dil}",
    )(h, h, h, h, h, bias)


def _combine_kernel(o1, o2, o3, l1, l2, l3, o_ref):
    a, b, c = l1[...], l2[...], l3[...]
    m = jnp.maximum(jnp.maximum(a, b), c)
    ea, eb, ec = jnp.exp(a - m), jnp.exp(b - m), jnp.exp(c - m)
    inv = 1.0 / (ea + eb + ec)
    o_ref[...] = (ea * inv * o1[...] + eb * inv * o2[...] + ec * inv * o3[...]).astype(o_ref.dtype)


def _combine(outs, lses, tm):
    t = outs[0].shape[0]
    spec = pl.BlockSpec((tm, B_OUT_W), lambda i: (i, 0))
    return pl.pallas_call(
        _combine_kernel,
        grid=(t // tm,),
        in_specs=[spec] * 6,
        out_specs=spec,
        out_shape=jax.ShapeDtypeStruct((t, B_OUT_W), BF16),
        compiler_params=_cparams(("parallel",)),
        name="combine_dilations",
    )(*outs, *lses)


def _kv_tail_kernel(k_ref, v_ref, o_ref):
    cw = k_ref.shape[1]
    o_ref[0, 0:cw, :] = k_ref[...].T
    o_ref[0, cw:2 * cw, :] = v_ref[...].T


def _kv_tail(h, *, batch, seq, win, cw, ck, cv):
    nblk = win // BLOCK
    base = (seq - win) // BLOCK
    per = seq // BLOCK
    return pl.pallas_call(
        _kv_tail_kernel,
        grid=(batch, nblk),
        in_specs=[pl.BlockSpec((BLOCK, cw), lambda b, i: (b * per + base + i, ck // cw)),
                  pl.BlockSpec((BLOCK, cw), lambda b, i: (b * per + base + i, cv // cw))],
        out_specs=pl.BlockSpec((1, 2 * cw, BLOCK), lambda b, i: (b, 0, i)),
        out_shape=jax.ShapeDtypeStruct((batch, 2 * cw, win), F32),
        compiler_params=_cparams(("parallel", "parallel")),
        name=f"kv_tail_w{win}_c{cw}",
    )(h, h)


def _step_kernel(*refs, n_heads, n_kv, win, has_sink, want_lse):
    c_ref, q_ref, knt_ref, vnt_ref, knf_ref, vnf_ref, bc_ref, bn_ref = refs[:8]
    pos = 8
    sink_ref = None
    if has_sink:
        sink_ref = refs[pos]
        pos += 1
    cout_ref, o_ref = refs[pos], refs[pos + 1]
    pos += 2
    lse_ref = None
    if want_lse:
        lse_ref = refs[pos]
        pos += 1
    o_scr = refs[pos]
    lse_scr = refs[pos + 1] if want_lse else None

    b = pl.program_id(0)
    kvw = n_kv * HEAD_DIM
    group = n_heads // n_kv
    lane = lax.broadcasted_iota(jnp.int32, (1, LANES), 1)

    lo = DEC_T * lax.rem(b, LANES // DEC_T)
    to_tail = lax.rem(2 * LANES - DEC_T - lo, LANES)
    keep = lane < LANES - DEC_T
    new_tail = jnp.concatenate([pltpu.roll(knf_ref[...], to_tail, 1), pltpu.roll(vnf_ref[...], to_tail, 1)], axis=0)
    nlb = win // LANES
    nxt = pltpu.roll(c_ref[0, :, 0:LANES], LANES - DEC_T, 1)
    for j in range(nlb):
        cur = nxt
        if j + 1 < nlb:
            nxt = pltpu.roll(c_ref[0, :, (j + 1) * LANES:(j + 2) * LANES], LANES - DEC_T, 1)
        else:
            nxt = new_tail
        cout_ref[0, :, j * LANES:(j + 1) * LANES] = jnp.where(keep, cur, nxt)

    pad = jnp.zeros((LANES - SUBLANES, HEAD_DIM), F32)
    for kv in range(n_kv):
        ksl = slice(kv * HEAD_DIM, (kv + 1) * HEAD_DIM)
        kt = c_ref[0, ksl, :].astype(BF16)
        vt = c_ref[0, kvw + kv * HEAD_DIM:kvw + (kv + 1) * HEAD_DIM, :].astype(BF16)
        kn = jnp.concatenate([knt_ref[:, ksl], pad], axis=0).astype(BF16)
        vn = jnp.concatenate([vnt_ref[:, ksl], pad], axis=0).astype(BF16)
        heads = range(kv * group, (kv + 1) * group)
        qs = jnp.concatenate([q_ref[:, h * HEAD_DIM:(h + 1) * HEAD_DIM] for h in heads], axis=0).astype(BF16)
        s = jnp.dot(qs, kt, preferred_element_type=F32) + bc_ref[kv]
        sn = lax.dot_general(qs, kn, (((1,), (1,)), ((), ())), preferred_element_type=F32) + bn_ref[kv]
        m = jnp.maximum(jnp.max(s, axis=-1, keepdims=True), jnp.max(sn, axis=-1, keepdims=True))
        if has_sink:
            sink = sink_ref[kv][:, 0:1]
            m = jnp.maximum(m, sink)
        p = jnp.exp(s - m)
        pn = jnp.exp(sn - m)
        l = jnp.sum(p, axis=-1, keepdims=True) + jnp.sum(pn, axis=-1, keepdims=True)
        if has_sink:
            l = l + jnp.exp(sink - m)
        inv = 1.0 / l
        o = lax.dot_general((p * inv).astype(BF16), vt, (((1,), (1,)), ((), ())), preferred_element_type=F32)
        o = o + jnp.dot((pn * inv).astype(BF16), vn, preferred_element_type=F32)
        lse = m + jnp.log(l)
        for g, h in enumerate(heads):
            hs = slice(h * HEAD_DIM, (h + 1) * HEAD_DIM)
            o_scr[:, hs] = o[g * SUBLANES:(g + 1) * SUBLANES, :]
            if want_lse:
                lse_scr[:, hs] = jnp.broadcast_to(lse[g * SUBLANES:(g + 1) * SUBLANES, :], (SUBLANES, HEAD_DIM))

    half = lax.rem(b, 2)
    mine = lax.broadcasted_iota(jnp.int32, (SUBLANES, 1), 0) // DEC_T == half

    @pl.when(half == 0)
    def _():
        o_ref[...] = jnp.where(mine, o_scr[...], 0.0)
        if want_lse:
            lse_ref[...] = jnp.where(mine, lse_scr[...], 0.0)

    @pl.when(half == 1)
    def _():
        o_ref[...] = jnp.where(mine, o_scr[...], o_ref[...])
        if want_lse:
            lse_ref[...] = jnp.where(mine, lse_scr[...], lse_ref[...])


def _step_attn(ct, hs, ht, bias_c, bias_n, sink, *, n_heads, n_kv, win, cq, ck, cv, want_lse):
    db = ct.shape[0]
    qw, kw = n_heads * HEAD_DIM, n_kv * HEAD_DIM
    g8 = (n_heads // n_kv) * SUBLANES
    per_tile = LANES // DEC_T
    per_blk = SUBLANES // DEC_T
    in_specs = [pl.BlockSpec((1, 2 * kw, win), lambda b: (b, 0, 0)),
                pl.BlockSpec((SUBLANES, qw), lambda b: (b // per_blk, cq // qw)),
                pl.BlockSpec((SUBLANES, kw), lambda b: (b // per_blk, ck // kw)),
                pl.BlockSpec((SUBLANES, kw), lambda b: (b // per_blk, cv // kw)),
                pl.BlockSpec((kw, LANES), lambda b: (ck // kw, b // per_tile)),
                pl.BlockSpec((kw, LANES), lambda b: (cv // kw, b // per_tile)),
                pl.BlockSpec((n_kv, g8, win), lambda b: (0, 0, 0)),
                pl.BlockSpec((n_kv, g8, LANES), lambda b: (0, 0, 0))]
    args = [ct, hs, hs, hs, ht, ht, bias_c, bias_n]
    if sink is not None:
        in_specs.append(pl.BlockSpec((n_kv, g8, LANES), lambda b: (0, 0, 0)))
        args.append(sink)
    o_spec = pl.BlockSpec((SUBLANES, qw), lambda b: (b // per_blk, 0))
    o_shape = jax.ShapeDtypeStruct((db * DEC_T, qw), F32)
    out_specs = [pl.BlockSpec((1, 2 * kw, win), lambda b: (b, 0, 0)), o_spec]
    out_shape = [jax.ShapeDtypeStruct(ct.shape, F32), o_shape]
    scratch = [pltpu.VMEM((SUBLANES, qw), F32)]
    if want_lse:
        out_specs.append(o_spec)
        out_shape.append(o_shape)
        scratch.append(pltpu.VMEM((SUBLANES, qw), F32))
    return pl.pallas_call(
        functools.partial(_step_kernel, n_heads=n_heads, n_kv=n_kv, win=win, has_sink=sink is not None, want_lse=want_lse),
        grid=(db,),
        in_specs=in_specs,
        out_specs=tuple(out_specs),
        out_shape=tuple(out_shape),
        scratch_shapes=scratch,
        compiler_params=_cparams(("arbitrary",)),
        name=f"step_attn_w{win}_h{n_heads}",
    )(*args)


def _prep_w_in(w):
    col = np.arange(HW)
    is_q = (col < C_KA) | ((col >= C_QB) & (col < C_KB))
    scale = jnp.asarray(np.where(is_q, SCALE, 1.0).astype(np.float32))
    pad = jnp.zeros((w.shape[0], C_QB - C_PAD), w.dtype)
    return (jnp.concatenate([w[:, :C_PAD], pad, w[:, C_PAD:]], axis=1) * scale).astype(BF16)


def _to_feature_major(cache):
    db, win = cache.shape[:2]
    return jnp.transpose(cache, (0, 2, 3, 4, 1)).reshape(db, -1, win)


def _from_feature_major(ct, heads):
    n, _, win = ct.shape
    return jnp.transpose(ct.reshape(n, 2, heads, HEAD_DIM, win), (0, 4, 1, 2, 3))[None]


def _tail_layers(x, h, oa, ob, w_oa, w_ob, w_out, ln1_g, ln1_b, w_ffn_in, w_ffn_out, ln2_g, ln2_b):
    t = x.shape[0]
    mixin = _gate_proj(oa, ob, h, w_oa, w_ob, min(512, t), 512)
    h1 = _out_ln(x, mixin, w_out, ln1_g, ln1_b, min(256, t))
    u = _ffn_in(h1, w_ffn_in, min(1024, t), 512)
    return _ffn_out(u, w_ffn_out, h1, ln2_g, ln2_b, min(512, t), 512)


def kernel(x_prompt, x_sample, cache_a, cache_b1, cache_b2, cache_b3, rel_bias, w_in, a_sink, w_oa, w_ob,
           w_out, ln1_g, ln1_b, w_ffn_in, w_ffn_out, ln2_g, ln2_b):
    batch, seq, _ = x_prompt.shape
    db, dt, _ = x_sample.shape
    assert dt == DEC_T and w_in.shape[0] == DEPTH and db % (SUBLANES // DEC_T) == 0
    tp, ts = batch * seq, db * dt

    w_in_b = _prep_w_in(w_in[0])
    weights = (w_oa[0].astype(BF16), w_ob[0].astype(BF16), w_out[0].astype(BF16), ln1_g, ln1_b,
               w_ffn_in[0].astype(BF16), w_ffn_out[0].astype(BF16), ln2_g, ln2_b)
    sink = a_sink[0].astype(F32)
    b_h0 = [A_Q_HEADS + g * B_HEADS for g in range(len(B_PATTERNS))]
    b_cols = [(C_QB + g * B_OUT_W, C_KB + g * B_OUT_W, C_VB + g * B_OUT_W) for g in range(len(B_PATTERNS))]

    xp = x_prompt.reshape(tp, D_MODEL)
    hp = _in_proj(xp, w_in_b, 1024, 1024)
    oa = _band_attn_a(hp, _band_bias(rel_bias, A_WINDOW - 1, 1, 0, A_Q_HEADS), sink, batch=batch, seq=seq)
    outs, lses = [], []
    for g, (win, dil) in enumerate(B_PATTERNS):
        cq, ck, cv = b_cols[g]
        o, lse = _dil_attn(hp, _band_bias(rel_bias, win // dil, dil, b_h0[g], B_HEADS), batch=batch, seq=seq, dil=dil,
                           cq=cq, ck=ck, cv=cv)
        outs.append(o)
        lses.append(lse)
    ob = _combine(outs, lses, 1024)
    yp = _tail_layers(xp, hp, oa, ob, *weights).reshape(batch, seq, D_MODEL)

    new_a_p = _from_feature_major(_kv_tail(hp, batch=batch, seq=seq, win=min(A_WINDOW, seq), cw=A_KV_W, ck=C_KA, cv=C_VA),
                                  A_KV_HEADS)
    new_b_p = [_from_feature_major(_kv_tail(hp, batch=batch, seq=seq, win=min(win, seq), cw=B_OUT_W,
                                            ck=b_cols[g][1], cv=b_cols[g][2]), B_HEADS)
               for g, (win, dil) in enumerate(B_PATTERNS)]

    xs = x_sample.reshape(ts, D_MODEL)
    hs = _in_proj(xs, w_in_b, ts, 1024)
    lanes = -(-ts // LANES) * LANES
    ht = jnp.pad(hs[:, :QKV_W].T, ((0, 0), (0, lanes - ts)))

    group_a = A_Q_HEADS // A_KV_HEADS
    sink_rows = jnp.broadcast_to(jnp.repeat(sink.reshape(A_KV_HEADS, group_a), SUBLANES, axis=1)[:, :, None],
                                 (A_KV_HEADS, group_a * SUBLANES, LANES))
    bc, bn = _step_bias(rel_bias, cache_a.shape[2], 1, 0, A_Q_HEADS, A_KV_HEADS, True)
    new_a_t, oa_s = _step_attn(_to_feature_major(cache_a[0]), hs, ht, bc, bn, sink_rows, n_heads=A_Q_HEADS,
                               n_kv=A_KV_HEADS, win=cache_a.shape[2], cq=C_QA, ck=C_KA, cv=C_VA, want_lse=False)
    new_b_s, outs, lses = [], [], []
    for g, ((win, dil), cache) in enumerate(zip(B_PATTERNS, (cache_b1, cache_b2, cache_b3))):
        cq, ck, cv = b_cols[g]
        bc, bn = _step_bias(rel_bias, cache.shape[2], dil, b_h0[g], B_HEADS, B_HEADS, False)
        new_t, o, lse = _step_attn(_to_feature_major(cache[0]), hs, ht, bc, bn, None, n_heads=B_HEADS, n_kv=B_HEADS,
                                   win=cache.shape[2], cq=cq, ck=ck, cv=cv, want_lse=True)
        new_b_s.append(_from_feature_major(new_t, B_HEADS))
        outs.append(o)
        lses.append(lse)
    ob_s = _combine(outs, lses, ts)
    ys = _tail_layers(xs, hs, oa_s.astype(BF16), ob_s, *weights).reshape(db, dt, D_MODEL)

    return (yp, ys, new_a_p, new_b_p[0], new_b_p[1], new_b_p[2],
            _from_feature_major(new_a_t, A_KV_HEADS), new_b_s[0], new_b_s[1], new_b_s[2])
```

```python
import functools
import math

import numpy as np
import jax
import jax.numpy as jnp
from jax import lax
from jax.experimental import pallas as pl
from jax.experimental.pallas import tpu as pltpu

F32 = jnp.float32
BF16 = jnp.bfloat16

D_MODEL = 2048
HEAD_DIM = 64
A_WINDOW = 128
A_Q_HEADS = 16
A_KV_HEADS = 2
B_PATTERNS = ((128, 1), (512, 4), (2048, 16))
B_HEADS = 8
NUM_BUCKETS = 32
REL_MAX_DIST = 2048
BLOCK = 128
D_FF = 5632
DEPTH = 1
ALPHA = (2 * DEPTH) ** 0.25
SCALE = HEAD_DIM ** -0.5
LN_EPS = 1e-5
NEG = -1e30
LANES = 128
SUBLANES = 8
DEC_T = 4

A_OUT_W = A_Q_HEADS * HEAD_DIM
A_KV_W = A_KV_HEADS * HEAD_DIM
B_OUT_W = B_HEADS * HEAD_DIM
C_QA, C_KA, C_VA, C_PAD, C_QB, C_KB, C_VB, C_GA, C_GB = 0, 1024, 1152, 1280, 1536, 3072, 4608, 6144, 8192
HW = 10240
QKV_W = C_GA
VMEM_LIMIT = 56 * 1024 * 1024


def _cparams(sem):
    return pltpu.CompilerParams(dimension_semantics=sem, vmem_limit_bytes=VMEM_LIMIT)


def _bucket_np(dist):
    d = np.maximum(np.asarray(dist, np.int64), 0)
    max_exact = NUM_BUCKETS // 2
    ratio = np.maximum(d, max_exact).astype(np.float32) / np.float32(max_exact)
    large = max_exact + (np.log(ratio) / np.float32(math.log(REL_MAX_DIST / max_exact))
                         * np.float32(NUM_BUCKETS - max_exact)).astype(np.int32)
    return np.where(d < max_exact, d, np.minimum(large, NUM_BUCKETS - 1)).astype(np.int32)


def _bias_by_dist(rel_bias, dists, h0, nh):
    return jnp.take(rel_bias[:, h0:h0 + nh], jnp.asarray(_bucket_np(dists)), axis=0).T.astype(F32)


def _band_bias(rel_bias, max_dist, dil, h0, nh):
    u = BLOCK - np.arange(2 * BLOCK)
    w = jnp.where(jnp.asarray((u >= 0) & (u <= max_dist))[None],
                  _bias_by_dist(rel_bias, np.maximum(u, 0) * dil, h0, nh), NEG)
    x = jnp.concatenate([w, jnp.full((nh, 1), NEG, F32)], axis=1)
    band = jnp.tile(x, (1, BLOCK))[:, :BLOCK * 2 * BLOCK].reshape(nh, BLOCK, 2 * BLOCK)
    return jnp.stack([band, jnp.where(jnp.asarray(np.arange(2 * BLOCK) < BLOCK), NEG, band)])


def _packed_rows(per_head, groups, pairs):
    lead, tail = per_head.shape[:-3], per_head.shape[-1]
    t = per_head.reshape(*lead, groups, pairs, 2, BLOCK, tail)
    t = jnp.swapaxes(t, -4, -3)
    return t.reshape(*lead, groups, 2 * pairs * BLOCK, tail)


def _step_bias(rel_bias, win, dil, h0, nh, n_kv, is_a):
    neg = lambda n: jnp.full((nh, n), NEG, F32)
    rows_c = []
    if is_a:
        rev = _bias_by_dist(rel_bias, np.arange(A_WINDOW - 1, -1, -1), h0, nh)
        for t in range(DEC_T):
            rows_c.append(jnp.concatenate([neg(t + 1), rev[:, :win - t - 1]], axis=1))
    elif dil == 1:
        rev = _bias_by_dist(rel_bias, np.arange(win, 0, -1), h0, nh)
        for t in range(DEC_T):
            rows_c.append(jnp.concatenate([neg(t), rev[:, :win - t]], axis=1))
    else:
        rev = _bias_by_dist(rel_bias, np.arange(win // dil, 0, -1) * dil, h0, nh)
        for t in range(DEC_T):
            slots = [rev[:, :, None] if r == t else jnp.full((nh, win // dil, 1), NEG, F32) for r in range(dil)]
            rows_c.append(jnp.concatenate(slots, axis=2).reshape(nh, win))
    bc = jnp.stack(rows_c * 2, axis=1)

    i = np.arange(SUBLANES)[:, None]
    j = np.arange(LANES)[None, :]
    dn = i % DEC_T - j % DEC_T
    vn = (j < SUBLANES) & (i // DEC_T == j // DEC_T) & (dn >= 0)
    if not is_a:
        vn &= dn % dil == 0
    near = _bias_by_dist(rel_bias, np.arange(DEC_T), h0, nh)
    bn = jnp.where(jnp.asarray(vn)[None], near[:, np.clip(dn, 0, DEC_T - 1)], NEG)
    g8 = (nh // n_kv) * SUBLANES
    return bc.reshape(n_kv, g8, win), bn.reshape(n_kv, g8, LANES)


def _inproj_kernel(x_ref, w_ref, o_ref, xb_ref):
    @pl.when(pl.program_id(1) == 0)
    def _():
        xb_ref[...] = x_ref[...].astype(BF16)

    o_ref[...] = jnp.dot(xb_ref[...], w_ref[...], preferred_element_type=F32)


def _in_proj(x, w, tm, tn):
    t, k = x.shape
    n = w.shape[1]
    return pl.pallas_call(
        _inproj_kernel,
        grid=(t // tm, n // tn),
        in_specs=[pl.BlockSpec((tm, k), lambda i, j: (i, 0)),
                  pl.BlockSpec((k, tn), lambda i, j: (0, j))],
        out_specs=pl.BlockSpec((tm, tn), lambda i, j: (i, j)),
        out_shape=jax.ShapeDtypeStruct((t, n), F32),
        scratch_shapes=[pltpu.VMEM((tm, k), BF16)],
        compiler_params=_cparams(("parallel", "arbitrary")),
        name="in_proj",
    )(x, w)


def _gate_proj_kernel(oa_ref, ob_ref, ga_ref, gb_ref, woa_ref, wob_ref, o_ref):
    pa = jnp.dot(oa_ref[...], woa_ref[...], preferred_element_type=F32)
    pb = jnp.dot(ob_ref[...], wob_ref[...], preferred_element_type=F32)
    mix = jax.nn.sigmoid(ga_ref[...]) * pa + jax.nn.sigmoid(gb_ref[...]) * pb
    o_ref[...] = mix.astype(o_ref.dtype)


def _gate_proj(oa, ob, h, w_oa, w_ob, tm, tn):
    t = oa.shape[0]
    nj = D_MODEL // tn
    return pl.pallas_call(
        _gate_proj_kernel,
        grid=(t // tm, nj),
        in_specs=[pl.BlockSpec((tm, A_OUT_W), lambda i, j: (i, 0)),
                  pl.BlockSpec((tm, B_OUT_W), lambda i, j: (i, 0)),
                  pl.BlockSpec((tm, tn), lambda i, j: (i, C_GA // tn + j)),
                  pl.BlockSpec((tm, tn), lambda i, j: (i, C_GB // tn + j)),
                  pl.BlockSpec((A_OUT_W, tn), lambda i, j: (0, j)),
                  pl.BlockSpec((B_OUT_W, tn), lambda i, j: (0, j))],
        out_specs=pl.BlockSpec((tm, tn), lambda i, j: (i, j)),
        out_shape=jax.ShapeDtypeStruct((t, D_MODEL), BF16),
        compiler_params=_cparams(("parallel", "arbitrary")),
        name="gate_proj",
    )(oa, ob, h, h, w_oa, w_ob)


def _layer_norm(z, g, b):
    mu = jnp.mean(z, axis=-1, keepdims=True)
    zc = z - mu
    var = jnp.mean(zc * zc, axis=-1, keepdims=True)
    return zc * lax.rsqrt(var + LN_EPS) * g + b


def _out_ln_kernel(x_ref, m_ref, w_ref, g_ref, b_ref, o_ref):
    mix = jnp.dot(m_ref[...], w_ref[...], preferred_element_type=F32)
    o_ref[...] = _layer_norm(ALPHA * x_ref[...] + mix, g_ref[...], b_ref[...])


def _out_ln(x, mixin, w_out, g, b, tm):
    t = x.shape[0]
    return pl.pallas_call(
        _out_ln_kernel,
        grid=(t // tm,),
        in_specs=[pl.BlockSpec((tm, D_MODEL), lambda i: (i, 0)),
                  pl.BlockSpec((tm, D_MODEL), lambda i: (i, 0)),
                  pl.BlockSpec((D_MODEL, D_MODEL), lambda i: (0, 0)),
                  pl.BlockSpec((1, D_MODEL), lambda i: (0, 0)),
                  pl.BlockSpec((1, D_MODEL), lambda i: (0, 0))],
        out_specs=pl.BlockSpec((tm, D_MODEL), lambda i: (i, 0)),
        out_shape=jax.ShapeDtypeStruct((t, D_MODEL), F32),
        compiler_params=_cparams(("parallel",)),
        name="out_ln1",
    )(x, mixin, w_out, g, b)


def _ffn_in_kernel(h_ref, wg_ref, wu_ref, o_ref, hb_ref):
    @pl.when(pl.program_id(1) == 0)
    def _():
        hb_ref[...] = h_ref[...].astype(BF16)

    hb = hb_ref[...]
    gate = jnp.dot(hb, wg_ref[...], preferred_element_type=F32)
    up = jnp.dot(hb, wu_ref[...], preferred_element_type=F32)
    o_ref[...] = (gate * jax.nn.sigmoid(gate) * up).astype(o_ref.dtype)


def _ffn_in(h1, w_ffn_in, tm, tn):
    t = h1.shape[0]
    nj = D_FF // tn
    return pl.pallas_call(
        _ffn_in_kernel,
        grid=(t // tm, nj),
        in_specs=[pl.BlockSpec((tm, D_MODEL), lambda i, j: (i, 0)),
                  pl.BlockSpec((D_MODEL, tn), lambda i, j: (0, j)),
                  pl.BlockSpec((D_MODEL, tn), lambda i, j: (0, nj + j))],
        out_specs=pl.BlockSpec((tm, tn), lambda i, j: (i, j)),
        out_shape=jax.ShapeDtypeStruct((t, D_FF), BF16),
        scratch_shapes=[pltpu.VMEM((tm, D_MODEL), BF16)],
        compiler_params=_cparams(("parallel", "arbitrary")),
        name="ffn_in",
    )(h1, w_ffn_in, w_ffn_in)


def _ffn_out_kernel(u_ref, w_ref, h_ref, g_ref, b_ref, o_ref, acc_ref):
    k = pl.program_id(1)

    @pl.when(k == 0)
    def _():
        acc_ref[...] = jnp.zeros_like(acc_ref)

    acc_ref[...] += jnp.dot(u_ref[...], w_ref[...], preferred_element_type=F32)

    @pl.when(k == pl.num_programs(1) - 1)
    def _():
        o_ref[...] = _layer_norm(ALPHA * h_ref[...] + acc_ref[...], g_ref[...], b_ref[...])


def _ffn_out(u, w_ffn_out, h1, g, b, tm, tk):
    t = u.shape[0]
    return pl.pallas_call(
        _ffn_out_kernel,
        grid=(t // tm, D_FF // tk),
        in_specs=[pl.BlockSpec((tm, tk), lambda i, k: (i, k)),
                  pl.BlockSpec((tk, D_MODEL), lambda i, k: (k, 0)),
                  pl.BlockSpec((tm, D_MODEL), lambda i, k: (i, 0)),
                  pl.BlockSpec((1, D_MODEL), lambda i, k: (0, 0)),
                  pl.BlockSpec((1, D_MODEL), lambda i, k: (0, 0))],
        out_specs=pl.BlockSpec((tm, D_MODEL), lambda i, k: (i, 0)),
        out_shape=jax.ShapeDtypeStruct((t, D_MODEL), F32),
        scratch_shapes=[pltpu.VMEM((tm, D_MODEL), F32)],
        compiler_params=_cparams(("parallel", "arbitrary")),
        name="ffn_out_ln2",
    )(u, w_ffn_out, h1, g, b)


def _head_lane_masks():
    lane = lax.broadcasted_iota(jnp.int32, (1, LANES), 1)
    return [lane < HEAD_DIM, lane >= HEAD_DIM]


def _packed_attention(q, k2b, v2, bias, masks, sinks):
    q2 = jnp.concatenate([jnp.where(mk, q, 0.0) for mk in masks], axis=0).astype(BF16)
    s = lax.dot_general(q2, k2b, (((1,), (1,)), ((), ())), preferred_element_type=F32) + bias
    m = jnp.max(s, axis=-1, keepdims=True)
    if sinks is not None:
        sink = jnp.where(lax.broadcasted_iota(jnp.int32, (2 * BLOCK, 1), 0) < BLOCK, sinks[0], sinks[1])
        m = jnp.maximum(m, sink)
    p = jnp.exp(s - m).astype(BF16)
    nd = jnp.zeros((BLOCK, 2 * LANES), F32)
    for hh, mk in enumerate(masks):
        ones = jnp.broadcast_to(jnp.where(mk, 1.0, 0.0), v2.shape)
        w = jnp.concatenate([jnp.where(mk, v2, 0.0), ones], axis=1).astype(BF16)
        nd = nd + jnp.dot(p[hh * BLOCK:(hh + 1) * BLOCK], w, preferred_element_type=F32)
    den = nd[:, LANES:]
    if sinks is not None:
        es = jnp.exp(sink - m)
        den = den + jnp.where(masks[0], es[:BLOCK], es[BLOCK:])
    return nd[:, :LANES] / den, jnp.where(masks[0], m[:BLOCK], m[BLOCK:]) + jnp.log(den)


def _band_attn_kernel(q_ref, kc_ref, kp_ref, vc_ref, vp_ref, bias_ref, sink_ref, o_ref):
    first = (pl.program_id(1) == 0).astype(jnp.int32)
    masks = _head_lane_masks()
    kx = jnp.concatenate([kp_ref[...], kc_ref[...]], axis=0)
    vx = jnp.concatenate([vp_ref[...], vc_ref[...]], axis=0)
    kr = pltpu.roll(kx, HEAD_DIM, 1)
    vr = pltpu.roll(vx, HEAD_DIM, 1)
    pairs_per_kv = (A_Q_HEADS // A_KV_HEADS) // 2
    for kv in range(A_KV_HEADS):
        k2 = jnp.where(masks[kv], kx, kr).astype(BF16)
        v2 = jnp.where(masks[kv], vx, vr)
        for pp in range(pairs_per_kv):
            pi = kv * pairs_per_kv + pp
            sl = slice(pi * LANES, (pi + 1) * LANES)
            out, _ = _packed_attention(q_ref[:, sl], k2, v2, bias_ref[first, pi], masks,
                                       (sink_ref[2 * pi], sink_ref[2 * pi + 1]))
            o_ref[:, sl] = out.astype(o_ref.dtype)


def _band_attn_a(h, bias, sink, *, batch, seq):
    nb = seq // BLOCK
    qw, kw = A_OUT_W, A_KV_W

    def cur(b, i):
        return b * nb + i

    def prev(b, i):
        return b * nb + jnp.maximum(i - 1, 0)

    return pl.pallas_call(
        _band_attn_kernel,
        grid=(batch, nb),
        in_specs=[pl.BlockSpec((BLOCK, qw), lambda b, i: (cur(b, i), C_QA // qw)),
                  pl.BlockSpec((BLOCK, kw), lambda b, i: (cur(b, i), C_KA // kw)),
                  pl.BlockSpec((BLOCK, kw), lambda b, i: (prev(b, i), C_KA // kw)),
                  pl.BlockSpec((BLOCK, kw), lambda b, i: (cur(b, i), C_VA // kw)),
                  pl.BlockSpec((BLOCK, kw), lambda b, i: (prev(b, i), C_VA // kw)),
                  pl.BlockSpec(bias.shape, lambda b, i: (0, 0, 0, 0)),
                  pl.BlockSpec(memory_space=pltpu.SMEM)],
        out_specs=pl.BlockSpec((BLOCK, qw), lambda b, i: (cur(b, i), 0)),
        out_shape=jax.ShapeDtypeStruct((batch * seq, qw), BF16),
        compiler_params=_cparams(("parallel", "arbitrary")),
        name="band_attn_a",
    )(h, h, h, h, h, bias, sink)


def _dil_attn_kernel(q_ref, kc_ref, kp_ref, vc_ref, vp_ref, bias_ref, o_ref, lse_ref, *, dil, nblk, unroll):
    first = (pl.program_id(2) == 0).astype(jnp.int32)
    masks = _head_lane_masks()

    def rows(r, j):
        return pl.ds(j * BLOCK * dil + r, BLOCK, stride=dil) if dil > 1 else pl.ds(j * BLOCK, BLOCK)

    def tile(r, j):
        cur = rows(r, j)
        if j == 0:
            kp, vp, dead = kp_ref[rows(r, 0), :], vp_ref[rows(r, 0), :], first
        else:
            kp, vp, dead = kc_ref[rows(r, j - 1), :], vc_ref[rows(r, j - 1), :], 0
        k2 = jnp.concatenate([kp, kc_ref[cur, :]], axis=0).astype(BF16)
        v2 = jnp.concatenate([vp, vc_ref[cur, :]], axis=0)
        out, lse = _packed_attention(q_ref[cur, :], k2, v2, bias_ref[dead, 0], masks, None)
        o_ref[cur, :] = out
        lse_ref[cur, :] = lse

    def body(it, carry):
        for u in range(unroll):
            for j in range(nblk):
                tile(it * unroll + u, j)
        return carry

    if dil == unroll:
        body(0, 0)
    else:
        lax.fori_loop(0, dil // unroll, body, 0)


def _dil_attn(h, bias, *, batch, seq, dil, nblk, cq, ck, cv):
    band = BLOCK * dil
    chunk = band * nblk
    nc = seq // chunk
    pairs = B_OUT_W // LANES

    def cur(b, p, i):
        return b * nc + i

    def prev(b, p, i):
        return jnp.maximum((b * nc + i) * nblk - 1, 0)

    o_spec = pl.BlockSpec((chunk, LANES), lambda b, p, i: (cur(b, p, i), p))
    o_shape = jax.ShapeDtypeStruct((batch * seq, B_OUT_W), F32)
    return pl.pallas_call(
        functools.partial(_dil_attn_kernel, dil=dil, nblk=nblk, unroll=min(dil, 4)),
        grid=(batch, pairs, nc),
        in_specs=[pl.BlockSpec((chunk, LANES), lambda b, p, i: (cur(b, p, i), cq // LANES + p)),
                  pl.BlockSpec((chunk, LANES), lambda b, p, i: (cur(b, p, i), ck // LANES + p)),
                  pl.BlockSpec((band, LANES), lambda b, p, i: (prev(b, p, i), ck // LANES + p)),
                  pl.BlockSpec((chunk, LANES), lambda b, p, i: (cur(b, p, i), cv // LANES + p)),
                  pl.BlockSpec((band, LANES), lambda b, p, i: (prev(b, p, i), cv // LANES + p)),
                  pl.BlockSpec((2, 1, 2 * BLOCK, 2 * BLOCK), lambda b, p, i: (0, p, 0, 0))],
        out_specs=(o_spec, o_spec),
        out_shape=(o_shape, o_shape),
        compiler_params=_cparams(("parallel", "parallel", "arbitrary")),
        name=f"dil_attn_d{dil}",
    )(h, h, h, h, h, bias)


def _combine_kernel(o1, o2, o3, l1, l2, l3, o_ref):
    a, b, c = l1[...], l2[...], l3[...]
    m = jnp.maximum(jnp.maximum(a, b), c)
    ea, eb, ec = jnp.exp(a - m), jnp.exp(b - m), jnp.exp(c - m)
    inv = 1.0 / (ea + eb + ec)
    o_ref[...] = (ea * inv * o1[...] + eb * inv * o2[...] + ec * inv * o3[...]).astype(o_ref.dtype)


def _combine(outs, lses, tm):
    t = outs[0].shape[0]
    spec = pl.BlockSpec((tm, B_OUT_W), lambda i: (i, 0))
    return pl.pallas_call(
        _combine_kernel,
        grid=(t // tm,),
        in_specs=[spec] * 6,
        out_specs=spec,
        out_shape=jax.ShapeDtypeStruct((t, B_OUT_W), BF16),
        compiler_params=_cparams(("parallel",)),
        name="combine_dilations",
    )(*outs, *lses)


def _kv_tail_kernel(k_ref, v_ref, o_ref):
    cw = k_ref.shape[1]
    o_ref[0, 0:cw, :] = k_ref[...].T
    o_ref[0, cw:2 * cw, :] = v_ref[...].T


def _kv_tail(h, *, batch, seq, win, cw, ck, cv):
    rows = min(win, 4 * BLOCK)
    nblk = win // rows
    base = (seq - win) // rows
    per = seq // rows
    return pl.pallas_call(
        _kv_tail_kernel,
        grid=(batch, nblk),
        in_specs=[pl.BlockSpec((rows, cw), lambda b, i: (b * per + base + i, ck // cw)),
                  pl.BlockSpec((rows, cw), lambda b, i: (b * per + base + i, cv // cw))],
        out_specs=pl.BlockSpec((1, 2 * cw, rows), lambda b, i: (b, 0, i)),
        out_shape=jax.ShapeDtypeStruct((batch, 2 * cw, win), F32),
        compiler_params=_cparams(("parallel", "parallel")),
        name=f"kv_tail_w{win}_c{cw}",
    )(h, h)


def _step_kernel(*refs, n_heads, n_kv, win, has_sink, want_lse):
    c_ref, q_ref, knt_ref, vnt_ref, knf_ref, vnf_ref, bc_ref, bn_ref = refs[:8]
    pos = 8
    sink_ref = None
    if has_sink:
        sink_ref = refs[pos]
        pos += 1
    cout_ref, o_ref = refs[pos], refs[pos + 1]
    pos += 2
    lse_ref = None
    if want_lse:
        lse_ref = refs[pos]
        pos += 1
    o_scr = refs[pos]
    lse_scr = refs[pos + 1] if want_lse else None

    b = pl.program_id(0)
    kvw = n_kv * HEAD_DIM
    group = n_heads // n_kv
    lane = lax.broadcasted_iota(jnp.int32, (1, LANES), 1)

    lo = DEC_T * lax.rem(b, LANES // DEC_T)
    to_tail = lax.rem(2 * LANES - DEC_T - lo, LANES)
    keep = lane < LANES - DEC_T
    new_tail = jnp.concatenate([pltpu.roll(knf_ref[...], to_tail, 1), pltpu.roll(vnf_ref[...], to_tail, 1)], axis=0)
    nlb = win // LANES
    nxt = pltpu.roll(c_ref[0, :, 0:LANES], LANES - DEC_T, 1)
    for j in range(nlb):
        cur = nxt
        if j + 1 < nlb:
            nxt = pltpu.roll(c_ref[0, :, (j + 1) * LANES:(j + 2) * LANES], LANES - DEC_T, 1)
        else:
            nxt = new_tail
        cout_ref[0, :, j * LANES:(j + 1) * LANES] = jnp.where(keep, cur, nxt)

    pad = jnp.zeros((LANES - SUBLANES, HEAD_DIM), F32)
    for kv in range(n_kv):
        ksl = slice(kv * HEAD_DIM, (kv + 1) * HEAD_DIM)
        kt = c_ref[0, ksl, :].astype(BF16)
        vt = c_ref[0, kvw + kv * HEAD_DIM:kvw + (kv + 1) * HEAD_DIM, :].astype(BF16)
        kn = jnp.concatenate([knt_ref[:, ksl], pad], axis=0).astype(BF16)
        vn = jnp.concatenate([vnt_ref[:, ksl], pad], axis=0).astype(BF16)
        heads = range(kv * group, (kv + 1) * group)
        qs = jnp.concatenate([q_ref[:, h * HEAD_DIM:(h + 1) * HEAD_DIM] for h in heads], axis=0).astype(BF16)
        s = jnp.dot(qs, kt, preferred_element_type=F32) + bc_ref[kv]
        sn = lax.dot_general(qs, kn, (((1,), (1,)), ((), ())), preferred_element_type=F32) + bn_ref[kv]
        m = jnp.maximum(jnp.max(s, axis=-1, keepdims=True), jnp.max(sn, axis=-1, keepdims=True))
        if has_sink:
            sink = sink_ref[kv][:, 0:1]
            m = jnp.maximum(m, sink)
        p = jnp.exp(s - m)
        pn = jnp.exp(sn - m)
        l = jnp.sum(p, axis=-1, keepdims=True) + jnp.sum(pn, axis=-1, keepdims=True)
        if has_sink:
            l = l + jnp.exp(sink - m)
        inv = 1.0 / l
        o = lax.dot_general((p * inv).astype(BF16), vt, (((1,), (1,)), ((), ())), preferred_element_type=F32)
        o = o + jnp.dot((pn * inv).astype(BF16), vn, preferred_element_type=F32)
        lse = m + jnp.log(l)
        for g, h in enumerate(heads):
            hs = slice(h * HEAD_DIM, (h + 1) * HEAD_DIM)
            o_scr[:, hs] = o[g * SUBLANES:(g + 1) * SUBLANES, :]
            if want_lse:
                lse_scr[:, hs] = jnp.broadcast_to(lse[g * SUBLANES:(g + 1) * SUBLANES, :], (SUBLANES, HEAD_DIM))

    half = lax.rem(b, 2)
    mine = lax.broadcasted_iota(jnp.int32, (SUBLANES, 1), 0) // DEC_T == half

    @pl.when(half == 0)
    def _():
        o_ref[...] = jnp.where(mine, o_scr[...], 0.0)
        if want_lse:
            lse_ref[...] = jnp.where(mine, lse_scr[...], 0.0)

    @pl.when(half == 1)
    def _():
        o_ref[...] = jnp.where(mine, o_scr[...], o_ref[...])
        if want_lse:
            lse_ref[...] = jnp.where(mine, lse_scr[...], lse_ref[...])


def _step_attn(ct, hs, ht, bias_c, bias_n, sink, *, n_heads, n_kv, win, cq, ck, cv, want_lse):
    db = ct.shape[0]
    qw, kw = n_heads * HEAD_DIM, n_kv * HEAD_DIM
    g8 = (n_heads // n_kv) * SUBLANES
    per_tile = LANES // DEC_T
    per_blk = SUBLANES // DEC_T
    in_specs = [pl.BlockSpec((1, 2 * kw, win), lambda b: (b, 0, 0)),
                pl.BlockSpec((SUBLANES, qw), lambda b: (b // per_blk, cq // qw)),
                pl.BlockSpec((SUBLANES, kw), lambda b: (b // per_blk, ck // kw)),
                pl.BlockSpec((SUBLANES, kw), lambda b: (b // per_blk, cv // kw)),
                pl.BlockSpec((kw, LANES), lambda b: (ck // kw, b // per_tile)),
                pl.BlockSpec((kw, LANES), lambda b: (cv // kw, b // per_tile)),
                pl.BlockSpec((n_kv, g8, win), lambda b: (0, 0, 0)),
                pl.BlockSpec((n_kv, g8, LANES), lambda b: (0, 0, 0))]
    args = [ct, hs, hs, hs, ht, ht, bias_c, bias_n]
    if sink is not None:
        in_specs.append(pl.BlockSpec((n_kv, g8, LANES), lambda b: (0, 0, 0)))
        args.append(sink)
    o_spec = pl.BlockSpec((SUBLANES, qw), lambda b: (b // per_blk, 0))
    o_shape = jax.ShapeDtypeStruct((db * DEC_T, qw), F32)
    out_specs = [pl.BlockSpec((1, 2 * kw, win), lambda b: (b, 0, 0)), o_spec]
    out_shape = [jax.ShapeDtypeStruct(ct.shape, F32), o_shape]
    scratch = [pltpu.VMEM((SUBLANES, qw), F32)]
    if want_lse:
        out_specs.append(o_spec)
        out_shape.append(o_shape)
        scratch.append(pltpu.VMEM((SUBLANES, qw), F32))
    return pl.pallas_call(
        functools.partial(_step_kernel, n_heads=n_heads, n_kv=n_kv, win=win, has_sink=sink is not None, want_lse=want_lse),
        grid=(db,),
        in_specs=in_specs,
        out_specs=tuple(out_specs),
        out_shape=tuple(out_shape),
        scratch_shapes=scratch,
        compiler_params=_cparams(("arbitrary",)),
        name=f"step_attn_w{win}_h{n_heads}",
    )(*args)


def _prep_w_in(w):
    col = np.arange(HW)
    is_q = (col < C_KA) | ((col >= C_QB) & (col < C_KB))
    scale = jnp.asarray(np.where(is_q, SCALE, 1.0).astype(np.float32))
    pad = jnp.zeros((w.shape[0], C_QB - C_PAD), w.dtype)
    return (jnp.concatenate([w[:, :C_PAD], pad, w[:, C_PAD:]], axis=1) * scale).astype(BF16)


def _to_feature_major(cache):
    db, win = cache.shape[:2]
    return jnp.transpose(cache, (0, 2, 3, 4, 1)).reshape(db, -1, win)


def _from_feature_major(ct, heads):
    n, _, win = ct.shape
    return jnp.transpose(ct.reshape(n, 2, heads, HEAD_DIM, win), (0, 4, 1, 2, 3))[None]


def _tail_layers(x, h, oa, ob, w_oa, w_ob, w_out, ln1_g, ln1_b, w_ffn_in, w_ffn_out, ln2_g, ln2_b):
    t = x.shape[0]
    mixin = _gate_proj(oa, ob, h, w_oa, w_ob, min(512, t), 1024)
    h1 = _out_ln(x, mixin, w_out, ln1_g, ln1_b, min(512, t))
    u = _ffn_in(h1, w_ffn_in, min(1024, t), 512)
    return _ffn_out(u, w_ffn_out, h1, ln2_g, ln2_b, min(512, t), 1408)


def kernel(x_prompt, x_sample, cache_a, cache_b1, cache_b2, cache_b3, rel_bias, w_in, a_sink, w_oa, w_ob,
           w_out, ln1_g, ln1_b, w_ffn_in, w_ffn_out, ln2_g, ln2_b):
    batch, seq, _ = x_prompt.shape
    db, dt, _ = x_sample.shape
    assert dt == DEC_T and w_in.shape[0] == DEPTH and db % (SUBLANES // DEC_T) == 0
    tp, ts = batch * seq, db * dt

    w_in_b = _prep_w_in(w_in[0])
    weights = (w_oa[0].astype(BF16), w_ob[0].astype(BF16), w_out[0].astype(BF16), ln1_g, ln1_b,
               w_ffn_in[0].astype(BF16), w_ffn_out[0].astype(BF16), ln2_g, ln2_b)
    sink = a_sink[0].astype(F32)
    b_h0 = [A_Q_HEADS + g * B_HEADS for g in range(len(B_PATTERNS))]
    b_cols = [(C_QB + g * B_OUT_W, C_KB + g * B_OUT_W, C_VB + g * B_OUT_W) for g in range(len(B_PATTERNS))]

    xp = x_prompt.reshape(tp, D_MODEL)
    hp = _in_proj(xp, w_in_b, 1024, 1024)
    bias_a = _packed_rows(_band_bias(rel_bias, A_WINDOW - 1, 1, 0, A_Q_HEADS), A_Q_HEADS // 2, 1)
    oa = _band_attn_a(hp, bias_a, sink, batch=batch, seq=seq)
    outs, lses = [], []
    for g, (win, dil) in enumerate(B_PATTERNS):
        cq, ck, cv = b_cols[g]
        bias_g = _packed_rows(_band_bias(rel_bias, win // dil, dil, b_h0[g], B_HEADS), B_HEADS // 2, 1)
        o, lse = _dil_attn(hp, bias_g, batch=batch, seq=seq, dil=dil, nblk=4 if dil == 1 else 1, cq=cq, ck=ck, cv=cv)
        outs.append(o)
        lses.append(lse)
    ob = _combine(outs, lses, 1024)
    yp = _tail_layers(xp, hp, oa, ob, *weights).reshape(batch, seq, D_MODEL)

    new_a_p = _from_feature_major(_kv_tail(hp, batch=batch, seq=seq, win=min(A_WINDOW, seq), cw=A_KV_W, ck=C_KA, cv=C_VA),
                                  A_KV_HEADS)
    new_b_p = [_from_feature_major(_kv_tail(hp, batch=batch, seq=seq, win=min(win, seq), cw=B_OUT_W,
                                            ck=b_cols[g][1], cv=b_cols[g][2]), B_HEADS)
               for g, (win, dil) in enumerate(B_PATTERNS)]

    xs = x_sample.reshape(ts, D_MODEL)
    hs = _in_proj(xs, w_in_b, ts, 1024)
    lanes = -(-ts // LANES) * LANES
    ht = jnp.pad(hs[:, :QKV_W].T, ((0, 0), (0, lanes - ts)))

    group_a = A_Q_HEADS // A_KV_HEADS
    sink_rows = jnp.broadcast_to(jnp.repeat(sink.reshape(A_KV_HEADS, group_a), SUBLANES, axis=1)[:, :, None],
                                 (A_KV_HEADS, group_a * SUBLANES, LANES))
    bc, bn = _step_bias(rel_bias, cache_a.shape[2], 1, 0, A_Q_HEADS, A_KV_HEADS, True)
    new_a_t, oa_s = _step_attn(_to_feature_major(cache_a[0]), hs, ht, bc, bn, sink_rows, n_heads=A_Q_HEADS,
                               n_kv=A_KV_HEADS, win=cache_a.shape[2], cq=C_QA, ck=C_KA, cv=C_VA, want_lse=False)
    new_b_s, outs, lses = [], [], []
    for g, ((win, dil), cache) in enumerate(zip(B_PATTERNS, (cache_b1, cache_b2, cache_b3))):
        cq, ck, cv = b_cols[g]
        bc, bn = _step_bias(rel_bias, cache.shape[2], dil, b_h0[g], B_HEADS, B_HEADS, False)
        new_t, o, lse = _step_attn(_to_feature_major(cache[0]), hs, ht, bc, bn, None, n_heads=B_HEADS, n_kv=B_HEADS,
                                   win=cache.shape[2], cq=cq, ck=ck, cv=cv, want_lse=True)
        new_b_s.append(_from_feature_major(new_t, B_HEADS))
        outs.append(o)
        lses.append(lse)
    ob_s = _combine(outs, lses, ts)
    ys = _tail_layers(xs, hs, oa_s.astype(BF16), ob_s, *weights).reshape(db, dt, D_MODEL)

    return (yp, ys, new_a_p, new_b_p[0], new_b_p[1], new_b_p[2],
            _from_feature_major(new_a_t, A_KV_HEADS), new_b_s[0], new_b_s[1], new_b_s[2])
```

```python
import functools
import math

import numpy as np
import jax
import jax.numpy as jnp
from jax import lax
from jax.experimental import pallas as pl
from jax.experimental.pallas import tpu as pltpu

F32 = jnp.float32
BF16 = jnp.bfloat16

D_MODEL = 2048
HEAD_DIM = 64
A_WINDOW = 128
A_Q_HEADS = 16
A_KV_HEADS = 2
B_PATTERNS = ((128, 1), (512, 4), (2048, 16))
B_HEADS = 8
NUM_BUCKETS = 32
REL_MAX_DIST = 2048
BLOCK = 128
D_FF = 5632
DEPTH = 1
ALPHA = (2 * DEPTH) ** 0.25
SCALE = HEAD_DIM ** -0.5
LN_EPS = 1e-5
NEG = -1e30
LANES = 128
SUBLANES = 8
DEC_T = 4

A_OUT_W = A_Q_HEADS * HEAD_DIM
A_KV_W = A_KV_HEADS * HEAD_DIM
B_OUT_W = B_HEADS * HEAD_DIM
C_QA, C_KA, C_VA, C_PAD, C_QB, C_KB, C_VB, C_GA, C_GB = 0, 1024, 1152, 1280, 1536, 3072, 4608, 6144, 8192
HW = 10240
QKV_W = C_GA
VMEM_LIMIT = 56 * 1024 * 1024


def _cparams(sem):
    return pltpu.CompilerParams(dimension_semantics=sem, vmem_limit_bytes=VMEM_LIMIT)


def _bucket_np(dist):
    d = np.maximum(np.asarray(dist, np.int64), 0)
    max_exact = NUM_BUCKETS // 2
    ratio = np.maximum(d, max_exact).astype(np.float32) / np.float32(max_exact)
    large = max_exact + (np.log(ratio) / np.float32(math.log(REL_MAX_DIST / max_exact))
                         * np.float32(NUM_BUCKETS - max_exact)).astype(np.int32)
    return np.where(d < max_exact, d, np.minimum(large, NUM_BUCKETS - 1)).astype(np.int32)


def _bias_by_dist(rel_bias, dists, h0, nh):
    return jnp.take(rel_bias[:, h0:h0 + nh], jnp.asarray(_bucket_np(dists)), axis=0).T.astype(F32)


def _band_bias(rel_bias, max_dist, dil, h0, nh):
    u = BLOCK - np.arange(2 * BLOCK)
    w = jnp.where(jnp.asarray((u >= 0) & (u <= max_dist))[None],
                  _bias_by_dist(rel_bias, np.maximum(u, 0) * dil, h0, nh), NEG)
    x = jnp.concatenate([w, jnp.full((nh, 1), NEG, F32)], axis=1)
    band = jnp.tile(x, (1, BLOCK))[:, :BLOCK * 2 * BLOCK].reshape(nh, BLOCK, 2 * BLOCK)
    return jnp.stack([band, jnp.where(jnp.asarray(np.arange(2 * BLOCK) < BLOCK), NEG, band)])


def _packed_rows(per_head, groups, pairs):
    lead, tail = per_head.shape[:-3], per_head.shape[-1]
    t = per_head.reshape(*lead, groups, pairs, 2, BLOCK, tail)
    t = jnp.swapaxes(t, -4, -3)
    return t.reshape(*lead, groups, 2 * pairs * BLOCK, tail)


def _step_bias(rel_bias, win, dil, h0, nh, n_kv, is_a):
    neg = lambda n: jnp.full((nh, n), NEG, F32)
    rows_c = []
    if is_a:
        rev = _bias_by_dist(rel_bias, np.arange(A_WINDOW - 1, -1, -1), h0, nh)
        for t in range(DEC_T):
            rows_c.append(jnp.concatenate([neg(t + 1), rev[:, :win - t - 1]], axis=1))
    elif dil == 1:
        rev = _bias_by_dist(rel_bias, np.arange(win, 0, -1), h0, nh)
        for t in range(DEC_T):
            rows_c.append(jnp.concatenate([neg(t), rev[:, :win - t]], axis=1))
    else:
        rev = _bias_by_dist(rel_bias, np.arange(win // dil, 0, -1) * dil, h0, nh)
        for t in range(DEC_T):
            slots = [rev[:, :, None] if r == t else jnp.full((nh, win // dil, 1), NEG, F32) for r in range(dil)]
            rows_c.append(jnp.concatenate(slots, axis=2).reshape(nh, win))
    bc = jnp.stack(rows_c * 2, axis=1)

    i = np.arange(SUBLANES)[:, None]
    j = np.arange(LANES)[None, :]
    dn = i % DEC_T - j % DEC_T
    vn = (j < SUBLANES) & (i // DEC_T == j // DEC_T) & (dn >= 0)
    if not is_a:
        vn &= dn % dil == 0
    near = _bias_by_dist(rel_bias, np.arange(DEC_T), h0, nh)
    bn = jnp.where(jnp.asarray(vn)[None], near[:, np.clip(dn, 0, DEC_T - 1)], NEG)
    g8 = (nh // n_kv) * SUBLANES
    return bc.reshape(n_kv, g8, win), bn.reshape(n_kv, g8, LANES)


def _inproj_kernel(x_ref, w_ref, o_ref, xb_ref):
    @pl.when(pl.program_id(1) == 0)
    def _():
        xb_ref[...] = x_ref[...].astype(BF16)

    o_ref[...] = jnp.dot(xb_ref[...], w_ref[...], preferred_element_type=F32)


def _in_proj(x, w, tm, tn):
    t, k = x.shape
    n = w.shape[1]
    return pl.pallas_call(
        _inproj_kernel,
        grid=(t // tm, n // tn),
        in_specs=[pl.BlockSpec((tm, k), lambda i, j: (i, 0)),
                  pl.BlockSpec((k, tn), lambda i, j: (0, j))],
        out_specs=pl.BlockSpec((tm, tn), lambda i, j: (i, j)),
        out_shape=jax.ShapeDtypeStruct((t, n), F32),
        scratch_shapes=[pltpu.VMEM((tm, k), BF16)],
        compiler_params=_cparams(("parallel", "arbitrary")),
        name="in_proj",
    )(x, w)


def _gate_proj_kernel(oa_ref, ob_ref, ga_ref, gb_ref, woa_ref, wob_ref, o_ref):
    pa = jnp.dot(oa_ref[...], woa_ref[...], preferred_element_type=F32)
    pb = jnp.dot(ob_ref[...], wob_ref[...], preferred_element_type=F32)
    mix = jax.nn.sigmoid(ga_ref[...]) * pa + jax.nn.sigmoid(gb_ref[...]) * pb
    o_ref[...] = mix.astype(o_ref.dtype)


def _gate_proj(oa, ob, h, w_oa, w_ob, tm, tn):
    t = oa.shape[0]
    nj = D_MODEL // tn
    return pl.pallas_call(
        _gate_proj_kernel,
        grid=(t // tm, nj),
        in_specs=[pl.BlockSpec((tm, A_OUT_W), lambda i, j: (i, 0)),
                  pl.BlockSpec((tm, B_OUT_W), lambda i, j: (i, 0)),
                  pl.BlockSpec((tm, tn), lambda i, j: (i, C_GA // tn + j)),
                  pl.BlockSpec((tm, tn), lambda i, j: (i, C_GB // tn + j)),
                  pl.BlockSpec((A_OUT_W, tn), lambda i, j: (0, j)),
                  pl.BlockSpec((B_OUT_W, tn), lambda i, j: (0, j))],
        out_specs=pl.BlockSpec((tm, tn), lambda i, j: (i, j)),
        out_shape=jax.ShapeDtypeStruct((t, D_MODEL), BF16),
        compiler_params=_cparams(("parallel", "arbitrary")),
        name="gate_proj",
    )(oa, ob, h, h, w_oa, w_ob)


def _layer_norm(z, g, b):
    mu = jnp.mean(z, axis=-1, keepdims=True)
    zc = z - mu
    var = jnp.mean(zc * zc, axis=-1, keepdims=True)
    return zc * lax.rsqrt(var + LN_EPS) * g + b


def _mm_res_ln_kernel(a_ref, w_ref, r_ref, g_ref, b_ref, o_ref):
    k = pl.program_id(1)
    part = jnp.dot(a_ref[...], w_ref[...], preferred_element_type=F32)

    @pl.when(k == 0)
    def _():
        o_ref[...] = part

    @pl.when(k > 0)
    def _():
        o_ref[...] += part

    @pl.when(k == pl.num_programs(1) - 1)
    def _():
        o_ref[...] = _layer_norm(ALPHA * r_ref[...] + o_ref[...], g_ref[...], b_ref[...])


def _mm_res_ln(a, w, res, g, b, tm, tk, name):
    t, kdim = a.shape
    return pl.pallas_call(
        _mm_res_ln_kernel,
        grid=(t // tm, kdim // tk),
        in_specs=[pl.BlockSpec((tm, tk), lambda i, k: (i, k)),
                  pl.BlockSpec((tk, D_MODEL), lambda i, k: (k, 0)),
                  pl.BlockSpec((tm, D_MODEL), lambda i, k: (i, 0)),
                  pl.BlockSpec((1, D_MODEL), lambda i, k: (0, 0)),
                  pl.BlockSpec((1, D_MODEL), lambda i, k: (0, 0))],
        out_specs=pl.BlockSpec((tm, D_MODEL), lambda i, k: (i, 0)),
        out_shape=jax.ShapeDtypeStruct((t, D_MODEL), F32),
        compiler_params=_cparams(("parallel", "arbitrary")),
        name=name,
    )(a, w, res, g, b)


def _ffn_in_kernel(h_ref, wg_ref, wu_ref, o_ref, hb_ref):
    @pl.when(pl.program_id(1) == 0)
    def _():
        hb_ref[...] = h_ref[...].astype(BF16)

    hb = hb_ref[...]
    gate = jnp.dot(hb, wg_ref[...], preferred_element_type=F32)
    up = jnp.dot(hb, wu_ref[...], preferred_element_type=F32)
    o_ref[...] = (gate * jax.nn.sigmoid(gate) * up).astype(o_ref.dtype)


def _ffn_in(h1, w_ffn_in, tm, tn):
    t = h1.shape[0]
    nj = D_FF // tn
    return pl.pallas_call(
        _ffn_in_kernel,
        grid=(t // tm, nj),
        in_specs=[pl.BlockSpec((tm, D_MODEL), lambda i, j: (i, 0)),
                  pl.BlockSpec((D_MODEL, tn), lambda i, j: (0, j)),
                  pl.BlockSpec((D_MODEL, tn), lambda i, j: (0, nj + j))],
        out_specs=pl.BlockSpec((tm, tn), lambda i, j: (i, j)),
        out_shape=jax.ShapeDtypeStruct((t, D_FF), BF16),
        scratch_shapes=[pltpu.VMEM((tm, D_MODEL), BF16)],
        compiler_params=_cparams(("parallel", "arbitrary")),
        name="ffn_in",
    )(h1, w_ffn_in, w_ffn_in)


def _head_lane_masks():
    lane = lax.broadcasted_iota(jnp.int32, (1, LANES), 1)
    return [lane < HEAD_DIM, lane >= HEAD_DIM]


def _packed_attention(q, k2b, v2, bias, masks, sinks):
    q2 = jnp.concatenate([jnp.where(mk, q, 0.0) for mk in masks], axis=0).astype(BF16)
    s = lax.dot_general(q2, k2b, (((1,), (1,)), ((), ())), preferred_element_type=F32) + bias
    m = jnp.max(s, axis=-1, keepdims=True)
    if sinks is not None:
        sink = jnp.where(lax.broadcasted_iota(jnp.int32, (2 * BLOCK, 1), 0) < BLOCK, sinks[0], sinks[1])
        m = jnp.maximum(m, sink)
    p = jnp.exp(s - m).astype(BF16)
    nd = jnp.zeros((BLOCK, 2 * LANES), F32)
    for hh, mk in enumerate(masks):
        ones = jnp.broadcast_to(jnp.where(mk, 1.0, 0.0), v2.shape)
        w = jnp.concatenate([jnp.where(mk, v2, 0.0), ones], axis=1).astype(BF16)
        nd = nd + jnp.dot(p[hh * BLOCK:(hh + 1) * BLOCK], w, preferred_element_type=F32)
    den = nd[:, LANES:]
    if sinks is not None:
        es = jnp.exp(sink - m)
        den = den + jnp.where(masks[0], es[:BLOCK], es[BLOCK:])
    return nd[:, :LANES] / den, jnp.where(masks[0], m[:BLOCK], m[BLOCK:]) + jnp.log(den)


def _band_attn_kernel(q_ref, kc_ref, kp_ref, vc_ref, vp_ref, bias_ref, sink_ref, o_ref):
    first = (pl.program_id(1) == 0).astype(jnp.int32)
    masks = _head_lane_masks()
    kx = jnp.concatenate([kp_ref[...], kc_ref[...]], axis=0)
    vx = jnp.concatenate([vp_ref[...], vc_ref[...]], axis=0)
    kr = pltpu.roll(kx, HEAD_DIM, 1)
    vr = pltpu.roll(vx, HEAD_DIM, 1)
    pairs_per_kv = (A_Q_HEADS // A_KV_HEADS) // 2
    for kv in range(A_KV_HEADS):
        k2 = jnp.where(masks[kv], kx, kr).astype(BF16)
        v2 = jnp.where(masks[kv], vx, vr)
        for pp in range(pairs_per_kv):
            pi = kv * pairs_per_kv + pp
            sl = slice(pi * LANES, (pi + 1) * LANES)
            out, _ = _packed_attention(q_ref[:, sl], k2, v2, bias_ref[first, pi], masks,
                                       (sink_ref[2 * pi], sink_ref[2 * pi + 1]))
            o_ref[:, sl] = out.astype(o_ref.dtype)


def _band_attn_a(h, bias, sink, *, batch, seq):
    nb = seq // BLOCK
    qw, kw = A_OUT_W, A_KV_W

    def cur(b, i):
        return b * nb + i

    def prev(b, i):
        return b * nb + jnp.maximum(i - 1, 0)

    return pl.pallas_call(
        _band_attn_kernel,
        grid=(batch, nb),
        in_specs=[pl.BlockSpec((BLOCK, qw), lambda b, i: (cur(b, i), C_QA // qw)),
                  pl.BlockSpec((BLOCK, kw), lambda b, i: (cur(b, i), C_KA // kw)),
                  pl.BlockSpec((BLOCK, kw), lambda b, i: (prev(b, i), C_KA // kw)),
                  pl.BlockSpec((BLOCK, kw), lambda b, i: (cur(b, i), C_VA // kw)),
                  pl.BlockSpec((BLOCK, kw), lambda b, i: (prev(b, i), C_VA // kw)),
                  pl.BlockSpec(bias.shape, lambda b, i: (0, 0, 0, 0)),
                  pl.BlockSpec(memory_space=pltpu.SMEM)],
        out_specs=pl.BlockSpec((BLOCK, qw), lambda b, i: (cur(b, i), 0)),
        out_shape=jax.ShapeDtypeStruct((batch * seq, qw), BF16),
        compiler_params=_cparams(("parallel", "arbitrary")),
        name="band_attn_a",
    )(h, h, h, h, h, bias, sink)


def _dil_attn_kernel(q_ref, kc_ref, kp_ref, vc_ref, vp_ref, bias_ref, o_ref, lse_ref, *, dil, nblk, unroll):
    first = (pl.program_id(2) == 0).astype(jnp.int32)
    masks = _head_lane_masks()

    def rows(r, j):
        return pl.ds(j * BLOCK * dil + r, BLOCK, stride=dil) if dil > 1 else pl.ds(j * BLOCK, BLOCK)

    def tile(r, j):
        cur = rows(r, j)
        if j == 0:
            kp, vp, dead = kp_ref[rows(r, 0), :], vp_ref[rows(r, 0), :], first
        else:
            kp, vp, dead = kc_ref[rows(r, j - 1), :], vc_ref[rows(r, j - 1), :], 0
        k2 = jnp.concatenate([kp, kc_ref[cur, :]], axis=0).astype(BF16)
        v2 = jnp.concatenate([vp, vc_ref[cur, :]], axis=0)
        out, lse = _packed_attention(q_ref[cur, :], k2, v2, bias_ref[dead, 0], masks, None)
        o_ref[cur, :] = out
        lse_ref[cur, :] = lse

    def body(it, carry):
        for u in range(unroll):
            for j in range(nblk):
                tile(it * unroll + u, j)
        return carry

    if dil == unroll:
        body(0, 0)
    else:
        lax.fori_loop(0, dil // unroll, body, 0)


def _dil_attn(h, bias, *, batch, seq, dil, nblk, cq, ck, cv):
    band = BLOCK * dil
    chunk = band * nblk
    nc = seq // chunk
    pairs = B_OUT_W // LANES

    def cur(b, p, i):
        return b * nc + i

    def prev(b, p, i):
        return jnp.maximum((b * nc + i) * nblk - 1, 0)

    o_spec = pl.BlockSpec((chunk, LANES), lambda b, p, i: (cur(b, p, i), p))
    o_shape = jax.ShapeDtypeStruct((batch * seq, B_OUT_W), F32)
    return pl.pallas_call(
        functools.partial(_dil_attn_kernel, dil=dil, nblk=nblk, unroll=min(dil, 4)),
        grid=(batch, pairs, nc),
        in_specs=[pl.BlockSpec((chunk, LANES), lambda b, p, i: (cur(b, p, i), cq // LANES + p)),
                  pl.BlockSpec((chunk, LANES), lambda b, p, i: (cur(b, p, i), ck // LANES + p)),
                  pl.BlockSpec((band, LANES), lambda b, p, i: (prev(b, p, i), ck // LANES + p)),
                  pl.BlockSpec((chunk, LANES), lambda b, p, i: (cur(b, p, i), cv // LANES + p)),
                  pl.BlockSpec((band, LANES), lambda b, p, i: (prev(b, p, i), cv // LANES + p)),
                  pl.BlockSpec((2, 1, 2 * BLOCK, 2 * BLOCK), lambda b, p, i: (0, p, 0, 0))],
        out_specs=(o_spec, o_spec),
        out_shape=(o_shape, o_shape),
        compiler_params=_cparams(("parallel", "parallel", "arbitrary")),
        name=f"dil_attn_d{dil}",
    )(h, h, h, h, h, bias)


def _combine_kernel(o1, o2, o3, l1, l2, l3, o_ref):
    a, b, c = l1[...], l2[...], l3[...]
    m = jnp.maximum(jnp.maximum(a, b), c)
    ea, eb, ec = jnp.exp(a - m), jnp.exp(b - m), jnp.exp(c - m)
    inv = 1.0 / (ea + eb + ec)
    o_ref[...] = (ea * inv * o1[...] + eb * inv * o2[...] + ec * inv * o3[...]).astype(o_ref.dtype)


def _combine(outs, lses, tm):
    t = outs[0].shape[0]
    spec = pl.BlockSpec((tm, B_OUT_W), lambda i: (i, 0))
    return pl.pallas_call(
        _combine_kernel,
        grid=(t // tm,),
        in_specs=[spec] * 6,
        out_specs=spec,
        out_shape=jax.ShapeDtypeStruct((t, B_OUT_W), BF16),
        compiler_params=_cparams(("parallel",)),
        name="combine_dilations",
    )(*outs, *lses)


def _kv_tail_kernel(k_ref, v_ref, o_ref):
    cw = k_ref.shape[1]
    o_ref[0, 0:cw, :] = k_ref[...].T
    o_ref[0, cw:2 * cw, :] = v_ref[...].T


def _kv_tail(h, *, batch, seq, win, cw, ck, cv):
    rows = min(win, 4 * BLOCK)
    nblk = win // rows
    base = (seq - win) // rows
    per = seq // rows
    return pl.pallas_call(
        _kv_tail_kernel,
        grid=(batch, nblk),
        in_specs=[pl.BlockSpec((rows, cw), lambda b, i: (b * per + base + i, ck // cw)),
                  pl.BlockSpec((rows, cw), lambda b, i: (b * per + base + i, cv // cw))],
        out_specs=pl.BlockSpec((1, 2 * cw, rows), lambda b, i: (b, 0, i)),
        out_shape=jax.ShapeDtypeStruct((batch, 2 * cw, win), F32),
        compiler_params=_cparams(("parallel", "parallel")),
        name=f"kv_tail_w{win}_c{cw}",
    )(h, h)


def _shift_window(c_ref, cout_ref, knf, vnf, lo):
    lane = lax.broadcasted_iota(jnp.int32, (1, LANES), 1)
    keep = lane < LANES - DEC_T
    to_tail = lax.rem(2 * LANES - DEC_T - lo, LANES)
    new_tail = jnp.concatenate([pltpu.roll(knf, to_tail, 1), pltpu.roll(vnf, to_tail, 1)], axis=0)
    nlb = c_ref.shape[2] // LANES
    nxt = pltpu.roll(c_ref[0, :, 0:LANES], LANES - DEC_T, 1)
    for j in range(nlb):
        cur = nxt
        if j + 1 < nlb:
            nxt = pltpu.roll(c_ref[0, :, (j + 1) * LANES:(j + 2) * LANES], LANES - DEC_T, 1)
        else:
            nxt = new_tail
        cout_ref[0, :, j * LANES:(j + 1) * LANES] = jnp.where(keep, cur, nxt)


def _pad_rows(x):
    return jnp.concatenate([x, jnp.zeros((LANES - SUBLANES, x.shape[1]), F32)], axis=0).astype(BF16)


def _step_attend_a(c_ref, q, knt, vnt, bc_ref, bn_ref, sink_ref, o_scr):
    kvw = A_KV_W
    group = A_Q_HEADS // A_KV_HEADS
    for kv in range(A_KV_HEADS):
        ksl = slice(kv * HEAD_DIM, (kv + 1) * HEAD_DIM)
        kt = c_ref[0, ksl, :].astype(BF16)
        vt = c_ref[0, kvw + kv * HEAD_DIM:kvw + (kv + 1) * HEAD_DIM, :].astype(BF16)
        heads = range(kv * group, (kv + 1) * group)
        qs = jnp.concatenate([q[:, h * HEAD_DIM:(h + 1) * HEAD_DIM] for h in heads], axis=0).astype(BF16)
        s = jnp.dot(qs, kt, preferred_element_type=F32) + bc_ref[kv]
        sn = lax.dot_general(qs, _pad_rows(knt[:, ksl]), (((1,), (1,)), ((), ())), preferred_element_type=F32) + bn_ref[kv]
        sink = sink_ref[kv][:, 0:1]
        m = jnp.maximum(jnp.maximum(jnp.max(s, axis=-1, keepdims=True), jnp.max(sn, axis=-1, keepdims=True)), sink)
        p = jnp.exp(s - m)
        pn = jnp.exp(sn - m)
        l = jnp.sum(p, axis=-1, keepdims=True) + jnp.sum(pn, axis=-1, keepdims=True) + jnp.exp(sink - m)
        o = lax.dot_general(p.astype(BF16), vt, (((1,), (1,)), ((), ())), preferred_element_type=F32)
        o = (o + jnp.dot(pn.astype(BF16), _pad_rows(vnt[:, ksl]), preferred_element_type=F32)) / l
        for g, h in enumerate(heads):
            o_scr[:, h * HEAD_DIM:(h + 1) * HEAD_DIM] = o[g * SUBLANES:(g + 1) * SUBLANES, :]


def _step_attend_b(c_ref, q, knt, vnt, bc_ref, bn_ref, head_lanes):
    kw = B_OUT_W
    qx = jnp.concatenate([jnp.where(mk, q, 0.0) for mk in head_lanes], axis=0).astype(BF16)
    kt = c_ref[0, 0:kw, :].astype(BF16)
    vt = c_ref[0, kw:2 * kw, :].astype(BF16)
    s = jnp.dot(qx, kt, preferred_element_type=F32) + bc_ref[...]
    sn = lax.dot_general(qx, _pad_rows(knt), (((1,), (1,)), ((), ())), preferred_element_type=F32) + bn_ref[...]
    m = jnp.maximum(jnp.max(s, axis=-1, keepdims=True), jnp.max(sn, axis=-1, keepdims=True))
    p = jnp.exp(s - m)
    pn = jnp.exp(sn - m)
    l = jnp.sum(p, axis=-1, keepdims=True) + jnp.sum(pn, axis=-1, keepdims=True)
    ox = lax.dot_general(p.astype(BF16), vt, (((1,), (1,)), ((), ())), preferred_element_type=F32)
    ox = ox + jnp.dot(pn.astype(BF16), _pad_rows(vnt), preferred_element_type=F32)
    o = jnp.zeros((SUBLANES, kw), F32)
    lrow = jnp.ones((SUBLANES, kw), F32)
    mrow = jnp.zeros((SUBLANES, kw), F32)
    for h, mk in enumerate(head_lanes):
        rows = slice(h * SUBLANES, (h + 1) * SUBLANES)
        o = jnp.where(mk, ox[rows], o)
        lrow = jnp.where(mk, l[rows], lrow)
        mrow = jnp.where(mk, m[rows], mrow)
    return o / lrow, mrow + jnp.log(lrow)


def _step_kernel(ca_ref, c1_ref, c2_ref, c3_ref, hq_ref, fa_ref, fb_ref,
                 bca_ref, bna_ref, sink_ref, bc1_ref, bn1_ref, bc2_ref, bn2_ref, bc3_ref, bn3_ref,
                 na_ref, n1_ref, n2_ref, n3_ref, oa_ref, ob_ref, oa_scr):
    b = pl.program_id(0)
    lo = DEC_T * lax.rem(b, LANES // DEC_T)

    _shift_window(ca_ref, na_ref, fa_ref[0:A_KV_W, :], fa_ref[A_KV_W:2 * A_KV_W, :], lo)
    groups = ((c1_ref, n1_ref, bc1_ref, bn1_ref), (c2_ref, n2_ref, bc2_ref, bn2_ref), (c3_ref, n3_ref, bc3_ref, bn3_ref))
    n_groups = len(groups)
    for g, (c_ref, n_ref, _, _) in enumerate(groups):
        _shift_window(c_ref, n_ref, fb_ref[g * B_OUT_W:(g + 1) * B_OUT_W, :],
                      fb_ref[(n_groups + g) * B_OUT_W:(n_groups + g + 1) * B_OUT_W, :], lo)

    _step_attend_a(ca_ref, hq_ref[:, C_QA:C_QA + A_OUT_W], hq_ref[:, C_KA:C_KA + A_KV_W], hq_ref[:, C_VA:C_VA + A_KV_W],
                   bca_ref, bna_ref, sink_ref, oa_scr)

    lane = lax.broadcasted_iota(jnp.int32, (1, B_OUT_W), 1)
    head_lanes = [lane // HEAD_DIM == h for h in range(B_HEADS)]
    outs, lses = [], []
    for g, (c_ref, _, bc_ref, bn_ref) in enumerate(groups):
        sl = lambda c0: slice(c0 + g * B_OUT_W, c0 + (g + 1) * B_OUT_W)
        o, lse = _step_attend_b(c_ref, hq_ref[:, sl(C_QB)], hq_ref[:, sl(C_KB)], hq_ref[:, sl(C_VB)], bc_ref, bn_ref,
                                head_lanes)
        outs.append(o)
        lses.append(lse)
    mx = functools.reduce(jnp.maximum, lses)
    es = [jnp.exp(lse - mx) for lse in lses]
    ob = sum(e * o for e, o in zip(es, outs)) / sum(es)

    half = lax.rem(b, 2)
    mine = lax.broadcasted_iota(jnp.int32, (SUBLANES, 1), 0) // DEC_T == half

    @pl.when(half == 0)
    def _():
        oa_ref[...] = jnp.where(mine, oa_scr[...], 0.0)
        ob_ref[...] = jnp.where(mine, ob, 0.0)

    @pl.when(half == 1)
    def _():
        oa_ref[...] = jnp.where(mine, oa_scr[...], oa_ref[...])
        ob_ref[...] = jnp.where(mine, ob, ob_ref[...])


def _step_attn(caches, hs, ht, tables):
    db = caches[0].shape[0]
    per_tile = LANES // DEC_T
    per_blk = SUBLANES // DEC_T
    n_kb = len(B_PATTERNS) * B_OUT_W
    cache_specs = [pl.BlockSpec((1,) + c.shape[1:], lambda b: (b, 0, 0)) for c in caches]
    in_specs = cache_specs + [
        pl.BlockSpec((SUBLANES, QKV_W), lambda b: (b // per_blk, 0)),
        pl.BlockSpec((2 * A_KV_W, LANES), lambda b: (C_KA // (2 * A_KV_W), b // per_tile)),
        pl.BlockSpec((2 * n_kb, LANES), lambda b: (C_KB // (2 * n_kb), b // per_tile)),
    ] + [pl.BlockSpec(t.shape, lambda b, nd=t.ndim: (0,) * nd) for t in tables]
    o_specs = [pl.BlockSpec((SUBLANES, A_OUT_W), lambda b: (b // per_blk, 0)),
               pl.BlockSpec((SUBLANES, B_OUT_W), lambda b: (b // per_blk, 0))]
    return pl.pallas_call(
        _step_kernel,
        grid=(db,),
        in_specs=in_specs,
        out_specs=tuple(cache_specs + o_specs),
        out_shape=tuple([jax.ShapeDtypeStruct(c.shape, F32) for c in caches]
                        + [jax.ShapeDtypeStruct((db * DEC_T, A_OUT_W), F32), jax.ShapeDtypeStruct((db * DEC_T, B_OUT_W), F32)]),
        scratch_shapes=[pltpu.VMEM((SUBLANES, A_OUT_W), F32)],
        compiler_params=_cparams(("arbitrary",)),
        name="step_attn",
    )(*caches, hs, ht, ht, *tables)


def _prep_w_in(w):
    col = np.arange(HW)
    is_q = (col < C_KA) | ((col >= C_QB) & (col < C_KB))
    scale = jnp.asarray(np.where(is_q, SCALE, 1.0).astype(np.float32))
    pad = jnp.zeros((w.shape[0], C_QB - C_PAD), w.dtype)
    return (jnp.concatenate([w[:, :C_PAD], pad, w[:, C_PAD:]], axis=1) * scale).astype(BF16)


def _to_feature_major(cache):
    db, win = cache.shape[:2]
    return jnp.transpose(cache, (0, 2, 3, 4, 1)).reshape(db, -1, win)


def _from_feature_major(ct, heads):
    n, _, win = ct.shape
    return jnp.transpose(ct.reshape(n, 2, heads, HEAD_DIM, win), (0, 4, 1, 2, 3))[None]


def _tail_layers(x, h, oa, ob, w_oa, w_ob, w_out, ln1_g, ln1_b, w_ffn_in, w_ffn_out, ln2_g, ln2_b):
    t = x.shape[0]
    tm = min(1024, t)
    mixin = _gate_proj(oa, ob, h, w_oa, w_ob, tm, 1024)
    h1 = _mm_res_ln(mixin, w_out, x, ln1_g, ln1_b, tm, 512, "out_ln1")
    u = _ffn_in(h1, w_ffn_in, tm, 512)
    return _mm_res_ln(u, w_ffn_out, h1, ln2_g, ln2_b, tm, 512, "ffn_out_ln2")


def kernel(x_prompt, x_sample, cache_a, cache_b1, cache_b2, cache_b3, rel_bias, w_in, a_sink, w_oa, w_ob,
           w_out, ln1_g, ln1_b, w_ffn_in, w_ffn_out, ln2_g, ln2_b):
    batch, seq, _ = x_prompt.shape
    db, dt, _ = x_sample.shape
    assert dt == DEC_T and w_in.shape[0] == DEPTH and db % (SUBLANES // DEC_T) == 0
    tp, ts = batch * seq, db * dt

    w_in_b = _prep_w_in(w_in[0])
    weights = (w_oa[0].astype(BF16), w_ob[0].astype(BF16), w_out[0].astype(BF16), ln1_g, ln1_b,
               w_ffn_in[0].astype(BF16), w_ffn_out[0].astype(BF16), ln2_g, ln2_b)
    sink = a_sink[0].astype(F32)
    b_h0 = [A_Q_HEADS + g * B_HEADS for g in range(len(B_PATTERNS))]
    b_cols = [(C_QB + g * B_OUT_W, C_KB + g * B_OUT_W, C_VB + g * B_OUT_W) for g in range(len(B_PATTERNS))]

    xp = x_prompt.reshape(tp, D_MODEL)
    hp = _in_proj(xp, w_in_b, 1024, 1024)
    bias_a = _packed_rows(_band_bias(rel_bias, A_WINDOW - 1, 1, 0, A_Q_HEADS), A_Q_HEADS // 2, 1)
    oa = _band_attn_a(hp, bias_a, sink, batch=batch, seq=seq)
    outs, lses = [], []
    for g, (win, dil) in enumerate(B_PATTERNS):
        cq, ck, cv = b_cols[g]
        bias_g = _packed_rows(_band_bias(rel_bias, win // dil, dil, b_h0[g], B_HEADS), B_HEADS // 2, 1)
        o, lse = _dil_attn(hp, bias_g, batch=batch, seq=seq, dil=dil, nblk=4 if dil == 1 else 1, cq=cq, ck=ck, cv=cv)
        outs.append(o)
        lses.append(lse)
    ob = _combine(outs, lses, 1024)
    yp = _tail_layers(xp, hp, oa, ob, *weights).reshape(batch, seq, D_MODEL)

    new_a_p = _from_feature_major(_kv_tail(hp, batch=batch, seq=seq, win=min(A_WINDOW, seq), cw=A_KV_W, ck=C_KA, cv=C_VA),
                                  A_KV_HEADS)
    new_b_p = [_from_feature_major(_kv_tail(hp, batch=batch, seq=seq, win=min(win, seq), cw=B_OUT_W,
                                            ck=b_cols[g][1], cv=b_cols[g][2]), B_HEADS)
               for g, (win, dil) in enumerate(B_PATTERNS)]

    xs = x_sample.reshape(ts, D_MODEL)
    hs = _in_proj(xs, w_in_b, ts, 1024)
    lanes = -(-ts // LANES) * LANES
    ht = jnp.pad(hs[:, :QKV_W].T, ((0, 0), (0, lanes - ts)))

    group_a = A_Q_HEADS // A_KV_HEADS
    sink_rows = jnp.broadcast_to(jnp.repeat(sink.reshape(A_KV_HEADS, group_a), SUBLANES, axis=1)[:, :, None],
                                 (A_KV_HEADS, group_a * SUBLANES, LANES))
    tables = list(_step_bias(rel_bias, cache_a.shape[2], 1, 0, A_Q_HEADS, A_KV_HEADS, True)) + [sink_rows]
    for g, ((win, dil), cache) in enumerate(zip(B_PATTERNS, (cache_b1, cache_b2, cache_b3))):
        bc, bn = _step_bias(rel_bias, cache.shape[2], dil, b_h0[g], B_HEADS, 1, False)
        tables += [bc[0], bn[0]]
    caches = [_to_feature_major(c[0]) for c in (cache_a, cache_b1, cache_b2, cache_b3)]
    new_a_t, new_b1_t, new_b2_t, new_b3_t, oa_s, ob_s = _step_attn(caches, hs, ht, tables)
    ys = _tail_layers(xs, hs, oa_s.astype(BF16), ob_s.astype(BF16), *weights).reshape(db, dt, D_MODEL)

    return (yp, ys, new_a_p, new_b_p[0], new_b_p[1], new_b_p[2],
            _from_feature_major(new_a_t, A_KV_HEADS), _from_feature_major(new_b1_t, B_HEADS),
            _from_feature_major(new_b2_t, B_HEADS), _from_feature_major(new_b3_t, B_HEADS))
```

```python
import functools
import math

import numpy as np
import jax
import jax.numpy as jnp
from jax import lax
from jax.experimental import pallas as pl
from jax.experimental.pallas import tpu as pltpu

F32 = jnp.float32
BF16 = jnp.bfloat16

D_MODEL = 2048
HEAD_DIM = 64
A_WINDOW = 128
A_Q_HEADS = 16
A_KV_HEADS = 2
B_PATTERNS = ((128, 1), (512, 4), (2048, 16))
B_HEADS = 8
NUM_BUCKETS = 32
REL_MAX_DIST = 2048
BLOCK = 128
D_FF = 5632
DEPTH = 1
ALPHA = (2 * DEPTH) ** 0.25
SCALE = HEAD_DIM ** -0.5
LN_EPS = 1e-5
NEG = -1e30
LANES = 128
SUBLANES = 8
DEC_T = 4

A_OUT_W = A_Q_HEADS * HEAD_DIM
A_KV_W = A_KV_HEADS * HEAD_DIM
B_OUT_W = B_HEADS * HEAD_DIM
C_QA, C_KA, C_VA, C_PAD, C_QB, C_KB, C_VB, C_GA, C_GB = 0, 1024, 1152, 1280, 1536, 3072, 4608, 6144, 8192
HW = 10240
QKV_W = C_GA
VMEM_LIMIT = 56 * 1024 * 1024


def _cparams(sem):
    return pltpu.CompilerParams(dimension_semantics=sem, vmem_limit_bytes=VMEM_LIMIT)


def _bucket_np(dist):
    d = np.maximum(np.asarray(dist, np.int64), 0)
    max_exact = NUM_BUCKETS // 2
    ratio = np.maximum(d, max_exact).astype(np.float32) / np.float32(max_exact)
    large = max_exact + (np.log(ratio) / np.float32(math.log(REL_MAX_DIST / max_exact))
                         * np.float32(NUM_BUCKETS - max_exact)).astype(np.int32)
    return np.where(d < max_exact, d, np.minimum(large, NUM_BUCKETS - 1)).astype(np.int32)


def _bias_by_dist(rel_bias, dists, h0, nh):
    return jnp.take(rel_bias[:, h0:h0 + nh], jnp.asarray(_bucket_np(dists)), axis=0).T.astype(F32)


def _band_bias(rel_bias, max_dist, dil, h0, nh):
    u = BLOCK - np.arange(2 * BLOCK)
    w = jnp.where(jnp.asarray((u >= 0) & (u <= max_dist))[None],
                  _bias_by_dist(rel_bias, np.maximum(u, 0) * dil, h0, nh), NEG)
    x = jnp.concatenate([w, jnp.full((nh, 1), NEG, F32)], axis=1)
    band = jnp.tile(x, (1, BLOCK))[:, :BLOCK * 2 * BLOCK].reshape(nh, BLOCK, 2 * BLOCK)
    return jnp.stack([band, jnp.where(jnp.asarray(np.arange(2 * BLOCK) < BLOCK), NEG, band)])


def _packed_rows(per_head, groups, pairs):
    lead, tail = per_head.shape[:-3], per_head.shape[-1]
    t = per_head.reshape(*lead, groups, pairs, 2, BLOCK, tail)
    t = jnp.swapaxes(t, -4, -3)
    return t.reshape(*lead, groups, 2 * pairs * BLOCK, tail)


def _step_bias(rel_bias, win, dil, h0, nh, n_kv, is_a):
    neg = lambda n: jnp.full((nh, n), NEG, F32)
    rows_c = []
    if is_a:
        rev = _bias_by_dist(rel_bias, np.arange(A_WINDOW - 1, -1, -1), h0, nh)
        for t in range(DEC_T):
            rows_c.append(jnp.concatenate([neg(t + 1), rev[:, :win - t - 1]], axis=1))
    elif dil == 1:
        rev = _bias_by_dist(rel_bias, np.arange(win, 0, -1), h0, nh)
        for t in range(DEC_T):
            rows_c.append(jnp.concatenate([neg(t), rev[:, :win - t]], axis=1))
    else:
        rev = _bias_by_dist(rel_bias, np.arange(win // dil, 0, -1) * dil, h0, nh)
        for t in range(DEC_T):
            slots = [rev[:, :, None] if r == t else jnp.full((nh, win // dil, 1), NEG, F32) for r in range(dil)]
            rows_c.append(jnp.concatenate(slots, axis=2).reshape(nh, win))
    bc = jnp.stack(rows_c * 2, axis=1)

    i = np.arange(SUBLANES)[:, None]
    j = np.arange(LANES)[None, :]
    dn = i % DEC_T - j % DEC_T
    vn = (j < SUBLANES) & (i // DEC_T == j // DEC_T) & (dn >= 0)
    if not is_a:
        vn &= dn % dil == 0
    near = _bias_by_dist(rel_bias, np.arange(DEC_T), h0, nh)
    bn = jnp.where(jnp.asarray(vn)[None], near[:, np.clip(dn, 0, DEC_T - 1)], NEG)
    g8 = (nh // n_kv) * SUBLANES
    return bc.reshape(n_kv, g8, win), bn.reshape(n_kv, g8, LANES)


def _inproj_kernel(x_ref, w_ref, o_ref, xb_ref):
    @pl.when(pl.program_id(1) == 0)
    def _():
        xb_ref[...] = x_ref[...].astype(BF16)

    o_ref[...] = jnp.dot(xb_ref[...], w_ref[...], preferred_element_type=F32)


def _in_proj(x, w, tm, tn):
    t, k = x.shape
    n = w.shape[1]
    return pl.pallas_call(
        _inproj_kernel,
        grid=(t // tm, n // tn),
        in_specs=[pl.BlockSpec((tm, k), lambda i, j: (i, 0)),
                  pl.BlockSpec((k, tn), lambda i, j: (0, j))],
        out_specs=pl.BlockSpec((tm, tn), lambda i, j: (i, j)),
        out_shape=jax.ShapeDtypeStruct((t, n), F32),
        scratch_shapes=[pltpu.VMEM((tm, k), BF16)],
        compiler_params=_cparams(("parallel", "arbitrary")),
        name="in_proj",
    )(x, w)


def _gate_proj_kernel(oa_ref, ob_ref, ga_ref, gb_ref, woa_ref, wob_ref, o_ref):
    pa = jnp.dot(oa_ref[...], woa_ref[...], preferred_element_type=F32)
    pb = jnp.dot(ob_ref[...], wob_ref[...], preferred_element_type=F32)
    sa = 0.5 * jnp.tanh(0.5 * ga_ref[...]) + 0.5
    sb = 0.5 * jnp.tanh(0.5 * gb_ref[...]) + 0.5
    o_ref[...] = (sa * pa + sb * pb).astype(o_ref.dtype)


def _gate_proj(oa, ob, h, w_oa, w_ob, tm, tn):
    t = oa.shape[0]
    nj = D_MODEL // tn
    return pl.pallas_call(
        _gate_proj_kernel,
        grid=(t // tm, nj),
        in_specs=[pl.BlockSpec((tm, A_OUT_W), lambda i, j: (i, 0)),
                  pl.BlockSpec((tm, B_OUT_W), lambda i, j: (i, 0)),
                  pl.BlockSpec((tm, tn), lambda i, j: (i, C_GA // tn + j)),
                  pl.BlockSpec((tm, tn), lambda i, j: (i, C_GB // tn + j)),
                  pl.BlockSpec((A_OUT_W, tn), lambda i, j: (0, j)),
                  pl.BlockSpec((B_OUT_W, tn), lambda i, j: (0, j))],
        out_specs=pl.BlockSpec((tm, tn), lambda i, j: (i, j)),
        out_shape=jax.ShapeDtypeStruct((t, D_MODEL), BF16),
        compiler_params=_cparams(("parallel", "arbitrary")),
        name="gate_proj",
    )(oa, ob, h, h, w_oa, w_ob)


def _layer_norm(z, g, b):
    mu = jnp.mean(z, axis=-1, keepdims=True)
    zc = z - mu
    var = jnp.mean(zc * zc, axis=-1, keepdims=True)
    return zc * lax.rsqrt(var + LN_EPS) * g + b


def _mm_res_ln_kernel(a_ref, w_ref, r_ref, g_ref, b_ref, o_ref, *, n_chunk):
    k = pl.program_id(1)
    nk = pl.num_programs(1)
    if n_chunk is None:
        o_ref[...] = _layer_norm(ALPHA * r_ref[...] + jnp.dot(a_ref[...], w_ref[...], preferred_element_type=F32),
                                 g_ref[...], b_ref[...])
        return

    @pl.when(k == 0)
    def _():
        o_ref[...] = jnp.zeros_like(o_ref)

    a = a_ref[...]
    for c in range(D_MODEL // n_chunk):
        cols = slice(c * n_chunk, (c + 1) * n_chunk)
        o_ref[:, cols] += jnp.dot(a, w_ref[:, cols], preferred_element_type=F32)

    @pl.when(k == nk - 1)
    def _():
        rows_per_pass = 256
        for r0 in range(0, o_ref.shape[0], rows_per_pass):
            rows = slice(r0, r0 + rows_per_pass)
            o_ref[rows, :] = _layer_norm(ALPHA * r_ref[rows, :] + o_ref[rows, :], g_ref[...], b_ref[...])


def _mm_res_ln(a, w, res, g, b, tm, tk, name):
    t, kdim = a.shape
    return pl.pallas_call(
        functools.partial(_mm_res_ln_kernel, n_chunk=None if tk == kdim else 512),
        grid=(t // tm, kdim // tk),
        in_specs=[pl.BlockSpec((tm, tk), lambda i, k: (i, k)),
                  pl.BlockSpec((tk, D_MODEL), lambda i, k: (k, 0)),
                  pl.BlockSpec((tm, D_MODEL), lambda i, k: (i, 0)),
                  pl.BlockSpec((1, D_MODEL), lambda i, k: (0, 0)),
                  pl.BlockSpec((1, D_MODEL), lambda i, k: (0, 0))],
        out_specs=pl.BlockSpec((tm, D_MODEL), lambda i, k: (i, 0)),
        out_shape=jax.ShapeDtypeStruct((t, D_MODEL), F32),
        compiler_params=_cparams(("parallel", "arbitrary")),
        name=name,
    )(a, w, res, g, b)


def _ffn_in_kernel(h_ref, wg_ref, wu_ref, o_ref, hb_ref):
    @pl.when(pl.program_id(1) == 0)
    def _():
        hb_ref[...] = h_ref[...].astype(BF16)

    hb = hb_ref[...]
    gate = jnp.dot(hb, wg_ref[...], preferred_element_type=F32)
    up = jnp.dot(hb, wu_ref[...], preferred_element_type=F32)
    o_ref[...] = (gate * jax.nn.sigmoid(gate) * up).astype(o_ref.dtype)


def _ffn_in(h1, w_ffn_in, tm, tn):
    t = h1.shape[0]
    nj = D_FF // tn
    return pl.pallas_call(
        _ffn_in_kernel,
        grid=(t // tm, nj),
        in_specs=[pl.BlockSpec((tm, D_MODEL), lambda i, j: (i, 0)),
                  pl.BlockSpec((D_MODEL, tn), lambda i, j: (0, j)),
                  pl.BlockSpec((D_MODEL, tn), lambda i, j: (0, nj + j))],
        out_specs=pl.BlockSpec((tm, tn), lambda i, j: (i, j)),
        out_shape=jax.ShapeDtypeStruct((t, D_FF), BF16),
        scratch_shapes=[pltpu.VMEM((tm, D_MODEL), BF16)],
        compiler_params=_cparams(("parallel", "arbitrary")),
        name="ffn_in",
    )(h1, w_ffn_in, w_ffn_in)


def _head_lane_masks():
    lane = lax.broadcasted_iota(jnp.int32, (1, LANES), 1)
    return [lane < HEAD_DIM, lane >= HEAD_DIM]


def _packed_attention(q, k2b, v2, bias, masks, sinks):
    q2 = jnp.concatenate([jnp.where(mk, q, 0.0) for mk in masks], axis=0).astype(BF16)
    s = lax.dot_general(q2, k2b, (((1,), (1,)), ((), ())), preferred_element_type=F32) + bias
    m = jnp.max(s, axis=-1, keepdims=True)
    if sinks is not None:
        sink = jnp.where(lax.broadcasted_iota(jnp.int32, (2 * BLOCK, 1), 0) < BLOCK, sinks[0], sinks[1])
        m = jnp.maximum(m, sink)
    p = jnp.exp(s - m).astype(BF16)
    nd = jnp.zeros((BLOCK, 2 * LANES), F32)
    for hh, mk in enumerate(masks):
        ones = jnp.broadcast_to(jnp.where(mk, 1.0, 0.0), v2.shape)
        w = jnp.concatenate([jnp.where(mk, v2, 0.0), ones], axis=1).astype(BF16)
        nd = nd + jnp.dot(p[hh * BLOCK:(hh + 1) * BLOCK], w, preferred_element_type=F32)
    den = nd[:, LANES:]
    if sinks is not None:
        es = jnp.exp(sink - m)
        den = den + jnp.where(masks[0], es[:BLOCK], es[BLOCK:])
    return nd[:, :LANES] / den, jnp.where(masks[0], m[:BLOCK], m[BLOCK:]) + jnp.log(den)


def _band_attn_kernel(q_ref, kc_ref, kp_ref, vc_ref, vp_ref, bias_ref, sink_ref, o_ref):
    first = (pl.program_id(1) == 0).astype(jnp.int32)
    masks = _head_lane_masks()
    pairs_per_kv = (A_Q_HEADS // A_KV_HEADS) // 2
    for j in range(q_ref.shape[0] // BLOCK):
        rows = slice(j * BLOCK, (j + 1) * BLOCK)
        if j == 0:
            kp, vp, dead = kp_ref[...], vp_ref[...], first
        else:
            kp, vp, dead = kc_ref[(j - 1) * BLOCK:j * BLOCK, :], vc_ref[(j - 1) * BLOCK:j * BLOCK, :], 0
        kx = jnp.concatenate([kp, kc_ref[rows, :]], axis=0)
        vx = jnp.concatenate([vp, vc_ref[rows, :]], axis=0)
        kr = pltpu.roll(kx, HEAD_DIM, 1)
        vr = pltpu.roll(vx, HEAD_DIM, 1)
        for kv in range(A_KV_HEADS):
            k2 = jnp.where(masks[kv], kx, kr).astype(BF16)
            v2 = jnp.where(masks[kv], vx, vr)
            for pp in range(pairs_per_kv):
                pi = kv * pairs_per_kv + pp
                sl = slice(pi * LANES, (pi + 1) * LANES)
                out, _ = _packed_attention(q_ref[rows, sl], k2, v2, bias_ref[dead, pi], masks,
                                           (sink_ref[2 * pi], sink_ref[2 * pi + 1]))
                o_ref[rows, sl] = out.astype(o_ref.dtype)


def _band_attn_a(h, bias, sink, *, batch, seq, nblk):
    chunk = BLOCK * nblk
    nc = seq // chunk
    qw, kw = A_OUT_W, A_KV_W

    def cur(b, i):
        return b * nc + i

    def prev(b, i):
        return jnp.maximum((b * nc + i) * nblk - 1, 0)

    return pl.pallas_call(
        _band_attn_kernel,
        grid=(batch, nc),
        in_specs=[pl.BlockSpec((chunk, qw), lambda b, i: (cur(b, i), C_QA // qw)),
                  pl.BlockSpec((chunk, kw), lambda b, i: (cur(b, i), C_KA // kw)),
                  pl.BlockSpec((BLOCK, kw), lambda b, i: (prev(b, i), C_KA // kw)),
                  pl.BlockSpec((chunk, kw), lambda b, i: (cur(b, i), C_VA // kw)),
                  pl.BlockSpec((BLOCK, kw), lambda b, i: (prev(b, i), C_VA // kw)),
                  pl.BlockSpec(bias.shape, lambda b, i: (0, 0, 0, 0)),
                  pl.BlockSpec(memory_space=pltpu.SMEM)],
        out_specs=pl.BlockSpec((chunk, qw), lambda b, i: (cur(b, i), 0)),
        out_shape=jax.ShapeDtypeStruct((batch * seq, qw), BF16),
        compiler_params=_cparams(("parallel", "arbitrary")),
        name="band_attn_a",
    )(h, h, h, h, h, bias, sink)


def _dil_attn_kernel(*refs, dil, nblk, unroll, has_prev):
    if has_prev:
        q_ref, kc_ref, vc_ref, kp_ref, vp_ref, bias_ref, o_ref, lse_ref = refs
    else:
        q_ref, kc_ref, vc_ref, bias_ref, o_ref, lse_ref = refs
    first = (pl.program_id(2) == 0).astype(jnp.int32)
    masks = _head_lane_masks()

    def rows(r, j):
        return pl.ds(j * BLOCK * dil + r, BLOCK, stride=dil) if dil > 1 else pl.ds(j * BLOCK, BLOCK)

    def tile(r, j):
        cur = rows(r, j)
        if not has_prev:
            k2, v2, bias = kc_ref[cur, :].astype(BF16), vc_ref[cur, :], bias_ref[0, 0][:, BLOCK:]
        else:
            if j == 0:
                kp, vp, dead = kp_ref[rows(r, 0), :], vp_ref[rows(r, 0), :], first
            else:
                kp, vp, dead = kc_ref[rows(r, j - 1), :], vc_ref[rows(r, j - 1), :], 0
            k2 = jnp.concatenate([kp, kc_ref[cur, :]], axis=0).astype(BF16)
            v2 = jnp.concatenate([vp, vc_ref[cur, :]], axis=0)
            bias = bias_ref[dead, 0]
        out, lse = _packed_attention(q_ref[cur, :], k2, v2, bias, masks, None)
        o_ref[cur, :] = out
        lse_ref[cur, :] = lse

    def body(it, carry):
        for u in range(unroll):
            for j in range(nblk):
                tile(it * unroll + u, j)
        return carry

    if dil == unroll:
        body(0, 0)
    else:
        lax.fori_loop(0, dil // unroll, body, 0)


def _dil_attn(h, bias, *, batch, seq, dil, nblk, cq, ck, cv):
    band = BLOCK * dil
    chunk = band * nblk
    nc = seq // chunk
    pairs = B_OUT_W // LANES

    def cur(b, p, i):
        return b * nc + i

    def prev(b, p, i):
        return jnp.maximum((b * nc + i) * nblk - 1, 0)

    o_spec = pl.BlockSpec((chunk, LANES), lambda b, p, i: (cur(b, p, i), p))
    o_shape = jax.ShapeDtypeStruct((batch * seq, B_OUT_W), F32)
    has_prev = seq > band
    in_specs = [pl.BlockSpec((chunk, LANES), lambda b, p, i: (cur(b, p, i), cq // LANES + p)),
                pl.BlockSpec((chunk, LANES), lambda b, p, i: (cur(b, p, i), ck // LANES + p)),
                pl.BlockSpec((chunk, LANES), lambda b, p, i: (cur(b, p, i), cv // LANES + p))]
    if has_prev:
        in_specs += [pl.BlockSpec((band, LANES), lambda b, p, i: (prev(b, p, i), ck // LANES + p)),
                     pl.BlockSpec((band, LANES), lambda b, p, i: (prev(b, p, i), cv // LANES + p))]
    in_specs.append(pl.BlockSpec((2, 1, 2 * BLOCK, 2 * BLOCK), lambda b, p, i: (0, p, 0, 0)))
    return pl.pallas_call(
        functools.partial(_dil_attn_kernel, dil=dil, nblk=nblk, unroll=min(dil, 8 // nblk), has_prev=has_prev),
        grid=(batch, pairs, nc),
        in_specs=in_specs,
        out_specs=(o_spec, o_spec),
        out_shape=(o_shape, o_shape),
        compiler_params=_cparams(("parallel", "parallel", "arbitrary")),
        name=f"dil_attn_d{dil}",
    )(*([h] * (len(in_specs) - 1)), bias)


def _combine_kernel(o1, o2, o3, l1, l2, l3, o_ref):
    a, b, c = l1[...], l2[...], l3[...]
    m = jnp.maximum(jnp.maximum(a, b), c)
    ea, eb, ec = jnp.exp(a - m), jnp.exp(b - m), jnp.exp(c - m)
    inv = 1.0 / (ea + eb + ec)
    o_ref[...] = (ea * inv * o1[...] + eb * inv * o2[...] + ec * inv * o3[...]).astype(o_ref.dtype)


def _combine(outs, lses, tm):
    t = outs[0].shape[0]
    spec = pl.BlockSpec((tm, B_OUT_W), lambda i: (i, 0))
    return pl.pallas_call(
        _combine_kernel,
        grid=(t // tm,),
        in_specs=[spec] * 6,
        out_specs=spec,
        out_shape=jax.ShapeDtypeStruct((t, B_OUT_W), BF16),
        compiler_params=_cparams(("parallel",)),
        name="combine_dilations",
    )(*outs, *lses)


def _kv_tail_kernel(k_ref, v_ref, o_ref):
    cw = k_ref.shape[1]
    o_ref[0, 0:cw, :] = k_ref[...].T
    o_ref[0, cw:2 * cw, :] = v_ref[...].T


def _kv_tail(h, *, batch, seq, win, cw, ck, cv):
    rows = min(win, 4 * BLOCK)
    nblk = win // rows
    base = (seq - win) // rows
    per = seq // rows
    return pl.pallas_call(
        _kv_tail_kernel,
        grid=(batch, nblk),
        in_specs=[pl.BlockSpec((rows, cw), lambda b, i: (b * per + base + i, ck // cw)),
                  pl.BlockSpec((rows, cw), lambda b, i: (b * per + base + i, cv // cw))],
        out_specs=pl.BlockSpec((1, 2 * cw, rows), lambda b, i: (b, 0, i)),
        out_shape=jax.ShapeDtypeStruct((batch, 2 * cw, win), F32),
        compiler_params=_cparams(("parallel", "parallel")),
        name=f"kv_tail_w{win}_c{cw}",
    )(h, h)


def _shift_window(c_ref, cout_ref, knf, vnf, lo):
    lane = lax.broadcasted_iota(jnp.int32, (1, LANES), 1)
    keep = lane < LANES - DEC_T
    to_tail = lax.rem(2 * LANES - DEC_T - lo, LANES)
    new_tail = jnp.concatenate([pltpu.roll(knf, to_tail, 1), pltpu.roll(vnf, to_tail, 1)], axis=0)
    nlb = c_ref.shape[2] // LANES
    nxt = pltpu.roll(c_ref[0, :, 0:LANES], LANES - DEC_T, 1)
    for j in range(nlb):
        cur = nxt
        if j + 1 < nlb:
            nxt = pltpu.roll(c_ref[0, :, (j + 1) * LANES:(j + 2) * LANES], LANES - DEC_T, 1)
        else:
            nxt = new_tail
        cout_ref[0, :, j * LANES:(j + 1) * LANES] = jnp.where(keep, cur, nxt)


def _pad_rows(x):
    return jnp.concatenate([x, jnp.zeros((LANES - SUBLANES, x.shape[1]), F32)], axis=0).astype(BF16)


def _step_attend_a(c_ref, q, knt, vnt, bc_ref, bn_ref, sink_ref, o_scr):
    kvw = A_KV_W
    group = A_Q_HEADS // A_KV_HEADS
    for kv in range(A_KV_HEADS):
        ksl = slice(kv * HEAD_DIM, (kv + 1) * HEAD_DIM)
        kt = c_ref[0, ksl, :].astype(BF16)
        vt = c_ref[0, kvw + kv * HEAD_DIM:kvw + (kv + 1) * HEAD_DIM, :].astype(BF16)
        heads = range(kv * group, (kv + 1) * group)
        qs = jnp.concatenate([q[:, h * HEAD_DIM:(h + 1) * HEAD_DIM] for h in heads], axis=0).astype(BF16)
        s = jnp.dot(qs, kt, preferred_element_type=F32) + bc_ref[kv]
        sn = lax.dot_general(qs, _pad_rows(knt[:, ksl]), (((1,), (1,)), ((), ())), preferred_element_type=F32) + bn_ref[kv]
        sink = sink_ref[kv][:, 0:1]
        m = jnp.maximum(jnp.maximum(jnp.max(s, axis=-1, keepdims=True), jnp.max(sn, axis=-1, keepdims=True)), sink)
        p = jnp.exp(s - m)
        pn = jnp.exp(sn - m)
        l = jnp.sum(p, axis=-1, keepdims=True) + jnp.sum(pn, axis=-1, keepdims=True) + jnp.exp(sink - m)
        o = lax.dot_general(p.astype(BF16), vt, (((1,), (1,)), ((), ())), preferred_element_type=F32)
        o = (o + jnp.dot(pn.astype(BF16), _pad_rows(vnt[:, ksl]), preferred_element_type=F32)) / l
        for g, h in enumerate(heads):
            o_scr[:, h * HEAD_DIM:(h + 1) * HEAD_DIM] = o[g * SUBLANES:(g + 1) * SUBLANES, :]


def _step_attend_b(c_ref, q, knt, vnt, bc_ref, bn_ref, head_lanes):
    kw = B_OUT_W
    qx = jnp.concatenate([jnp.where(mk, q, 0.0) for mk in head_lanes], axis=0).astype(BF16)
    kt = c_ref[0, 0:kw, :].astype(BF16)
    vt = c_ref[0, kw:2 * kw, :].astype(BF16)
    s = jnp.dot(qx, kt, preferred_element_type=F32) + bc_ref[...]
    sn = lax.dot_general(qx, _pad_rows(knt), (((1,), (1,)), ((), ())), preferred_element_type=F32) + bn_ref[...]
    m = jnp.maximum(jnp.max(s, axis=-1, keepdims=True), jnp.max(sn, axis=-1, keepdims=True))
    p = jnp.exp(s - m)
    pn = jnp.exp(sn - m)
    l = jnp.sum(p, axis=-1, keepdims=True) + jnp.sum(pn, axis=-1, keepdims=True)
    ox = lax.dot_general(p.astype(BF16), vt, (((1,), (1,)), ((), ())), preferred_element_type=F32)
    ox = ox + jnp.dot(pn.astype(BF16), _pad_rows(vnt), preferred_element_type=F32)
    o = jnp.zeros((SUBLANES, kw), F32)
    lrow = jnp.ones((SUBLANES, kw), F32)
    mrow = jnp.zeros((SUBLANES, kw), F32)
    for h, mk in enumerate(head_lanes):
        rows = slice(h * SUBLANES, (h + 1) * SUBLANES)
        o = jnp.where(mk, ox[rows], o)
        lrow = jnp.where(mk, l[rows], lrow)
        mrow = jnp.where(mk, m[rows], mrow)
    return o / lrow, mrow + jnp.log(lrow)


def _step_kernel(ca_ref, c1_ref, c2_ref, c3_ref, hq_ref, fa_ref, fb_ref,
                 bca_ref, bna_ref, sink_ref, bc1_ref, bn1_ref, bc2_ref, bn2_ref, bc3_ref, bn3_ref,
                 na_ref, n1_ref, n2_ref, n3_ref, oa_ref, ob_ref, oa_scr):
    b = pl.program_id(0)
    lo = DEC_T * lax.rem(b, LANES // DEC_T)

    _shift_window(ca_ref, na_ref, fa_ref[0:A_KV_W, :], fa_ref[A_KV_W:2 * A_KV_W, :], lo)
    groups = ((c1_ref, n1_ref, bc1_ref, bn1_ref), (c2_ref, n2_ref, bc2_ref, bn2_ref), (c3_ref, n3_ref, bc3_ref, bn3_ref))
    n_groups = len(groups)
    for g, (c_ref, n_ref, _, _) in enumerate(groups):
        _shift_window(c_ref, n_ref, fb_ref[g * B_OUT_W:(g + 1) * B_OUT_W, :],
                      fb_ref[(n_groups + g) * B_OUT_W:(n_groups + g + 1) * B_OUT_W, :], lo)

    _step_attend_a(ca_ref, hq_ref[:, C_QA:C_QA + A_OUT_W], hq_ref[:, C_KA:C_KA + A_KV_W], hq_ref[:, C_VA:C_VA + A_KV_W],
                   bca_ref, bna_ref, sink_ref, oa_scr)

    lane = lax.broadcasted_iota(jnp.int32, (1, B_OUT_W), 1)
    head_lanes = [lane // HEAD_DIM == h for h in range(B_HEADS)]
    outs, lses = [], []
    for g, (c_ref, _, bc_ref, bn_ref) in enumerate(groups):
        sl = lambda c0: slice(c0 + g * B_OUT_W, c0 + (g + 1) * B_OUT_W)
        o, lse = _step_attend_b(c_ref, hq_ref[:, sl(C_QB)], hq_ref[:, sl(C_KB)], hq_ref[:, sl(C_VB)], bc_ref, bn_ref,
                                head_lanes)
        outs.append(o)
        lses.append(lse)
    mx = functools.reduce(jnp.maximum, lses)
    es = [jnp.exp(lse - mx) for lse in lses]
    ob = sum(e * o for e, o in zip(es, outs)) / sum(es)

    half = lax.rem(b, 2)
    mine = lax.broadcasted_iota(jnp.int32, (SUBLANES, 1), 0) // DEC_T == half

    @pl.when(half == 0)
    def _():
        oa_ref[...] = jnp.where(mine, oa_scr[...], 0.0)
        ob_ref[...] = jnp.where(mine, ob, 0.0)

    @pl.when(half == 1)
    def _():
        oa_ref[...] = jnp.where(mine, oa_scr[...], oa_ref[...])
        ob_ref[...] = jnp.where(mine, ob, ob_ref[...])


def _step_attn(caches, hs, ht, tables):
    db = caches[0].shape[0]
    per_tile = LANES // DEC_T
    per_blk = SUBLANES // DEC_T
    n_kb = len(B_PATTERNS) * B_OUT_W
    cache_specs = [pl.BlockSpec((1,) + c.shape[1:], lambda b: (b, 0, 0)) for c in caches]
    in_specs = cache_specs + [
        pl.BlockSpec((SUBLANES, QKV_W), lambda b: (b // per_blk, 0)),
        pl.BlockSpec((2 * A_KV_W, LANES), lambda b: (C_KA // (2 * A_KV_W), b // per_tile)),
        pl.BlockSpec((2 * n_kb, LANES), lambda b: (C_KB // (2 * n_kb), b // per_tile)),
    ] + [pl.BlockSpec(t.shape, lambda b, nd=t.ndim: (0,) * nd) for t in tables]
    o_specs = [pl.BlockSpec((SUBLANES, A_OUT_W), lambda b: (b // per_blk, 0)),
               pl.BlockSpec((SUBLANES, B_OUT_W), lambda b: (b // per_blk, 0))]
    return pl.pallas_call(
        _step_kernel,
        grid=(db,),
        in_specs=in_specs,
        out_specs=tuple(cache_specs + o_specs),
        out_shape=tuple([jax.ShapeDtypeStruct(c.shape, F32) for c in caches]
                        + [jax.ShapeDtypeStruct((db * DEC_T, A_OUT_W), F32), jax.ShapeDtypeStruct((db * DEC_T, B_OUT_W), F32)]),
        scratch_shapes=[pltpu.VMEM((SUBLANES, A_OUT_W), F32)],
        compiler_params=_cparams(("arbitrary",)),
        name="step_attn",
    )(*caches, hs, ht, ht, *tables)


def _prep_w_in(w):
    col = np.arange(HW)
    is_q = (col < C_KA) | ((col >= C_QB) & (col < C_KB))
    scale = jnp.asarray(np.where(is_q, SCALE, 1.0).astype(np.float32))
    pad = jnp.zeros((w.shape[0], C_QB - C_PAD), w.dtype)
    return (jnp.concatenate([w[:, :C_PAD], pad, w[:, C_PAD:]], axis=1) * scale).astype(BF16)


def _to_feature_major(cache):
    db, win = cache.shape[:2]
    return jnp.transpose(cache, (0, 2, 3, 4, 1)).reshape(db, -1, win)


def _from_feature_major(ct, heads):
    n, _, win = ct.shape
    return jnp.transpose(ct.reshape(n, 2, heads, HEAD_DIM, win), (0, 4, 1, 2, 3))[None]


def _tail_layers(x, h, oa, ob, w_oa, w_ob, w_out, ln1_g, ln1_b, w_ffn_in, w_ffn_out, ln2_g, ln2_b):
    t = x.shape[0]
    tm = min(1024, t)
    mixin = _gate_proj(oa, ob, h, w_oa, w_ob, tm, 1024)
    h1 = _mm_res_ln(mixin, w_out, x, ln1_g, ln1_b, min(512, t), D_MODEL, "out_ln1")
    u = _ffn_in(h1, w_ffn_in, tm, 512)
    return _mm_res_ln(u, w_ffn_out, h1, ln2_g, ln2_b, tm, D_FF // 4, "ffn_out_ln2")


def kernel(x_prompt, x_sample, cache_a, cache_b1, cache_b2, cache_b3, rel_bias, w_in, a_sink, w_oa, w_ob,
           w_out, ln1_g, ln1_b, w_ffn_in, w_ffn_out, ln2_g, ln2_b):
    batch, seq, _ = x_prompt.shape
    db, dt, _ = x_sample.shape
    assert dt == DEC_T and w_in.shape[0] == DEPTH and db % (SUBLANES // DEC_T) == 0
    tp, ts = batch * seq, db * dt

    w_in_b = _prep_w_in(w_in[0])
    weights = (w_oa[0].astype(BF16), w_ob[0].astype(BF16), w_out[0].astype(BF16), ln1_g, ln1_b,
               w_ffn_in[0].astype(BF16), w_ffn_out[0].astype(BF16), ln2_g, ln2_b)
    sink = a_sink[0].astype(F32)
    b_h0 = [A_Q_HEADS + g * B_HEADS for g in range(len(B_PATTERNS))]
    b_cols = [(C_QB + g * B_OUT_W, C_KB + g * B_OUT_W, C_VB + g * B_OUT_W) for g in range(len(B_PATTERNS))]

    xp = x_prompt.reshape(tp, D_MODEL)
    hp = _in_proj(xp, w_in_b, 1024, 1024)
    bias_a = _packed_rows(_band_bias(rel_bias, A_WINDOW - 1, 1, 0, A_Q_HEADS), A_Q_HEADS // 2, 1)
    oa = _band_attn_a(hp, bias_a, sink, batch=batch, seq=seq, nblk=2)
    outs, lses = [], []
    for g, (win, dil) in enumerate(B_PATTERNS):
        cq, ck, cv = b_cols[g]
        bias_g = _packed_rows(_band_bias(rel_bias, win // dil, dil, b_h0[g], B_HEADS), B_HEADS // 2, 1)
        o, lse = _dil_attn(hp, bias_g, batch=batch, seq=seq, dil=dil, nblk={1: 8, 4: 2, 16: 1}[dil], cq=cq, ck=ck, cv=cv)
        outs.append(o)
        lses.append(lse)
    ob = _combine(outs, lses, 1024)
    yp = _tail_layers(xp, hp, oa, ob, *weights).reshape(batch, seq, D_MODEL)

    new_a_p = _from_feature_major(_kv_tail(hp, batch=batch, seq=seq, win=min(A_WINDOW, seq), cw=A_KV_W, ck=C_KA, cv=C_VA),
                                  A_KV_HEADS)
    new_b_p = [_from_feature_major(_kv_tail(hp, batch=batch, seq=seq, win=min(win, seq), cw=B_OUT_W,
                                            ck=b_cols[g][1], cv=b_cols[g][2]), B_HEADS)
               for g, (win, dil) in enumerate(B_PATTERNS)]

    xs = x_sample.reshape(ts, D_MODEL)
    hs = _in_proj(xs, w_in_b, ts, 1024)
    lanes = -(-ts // LANES) * LANES
    ht = jnp.pad(hs[:, :QKV_W].T, ((0, 0), (0, lanes - ts)))

    group_a = A_Q_HEADS // A_KV_HEADS
    sink_rows = jnp.broadcast_to(jnp.repeat(sink.reshape(A_KV_HEADS, group_a), SUBLANES, axis=1)[:, :, None],
                                 (A_KV_HEADS, group_a * SUBLANES, LANES))
    tables = list(_step_bias(rel_bias, cache_a.shape[2], 1, 0, A_Q_HEADS, A_KV_HEADS, True)) + [sink_rows]
    for g, ((win, dil), cache) in enumerate(zip(B_PATTERNS, (cache_b1, cache_b2, cache_b3))):
        bc, bn = _step_bias(rel_bias, cache.shape[2], dil, b_h0[g], B_HEADS, 1, False)
        tables += [bc[0], bn[0]]
    caches = [_to_feature_major(c[0]) for c in (cache_a, cache_b1, cache_b2, cache_b3)]
    new_a_t, new_b1_t, new_b2_t, new_b3_t, oa_s, ob_s = _step_attn(caches, hs, ht, tables)
    ys = _tail_layers(xs, hs, oa_s.astype(BF16), ob_s.astype(BF16), *weights).reshape(db, dt, D_MODEL)

    return (yp, ys, new_a_p, new_b_p[0], new_b_p[1], new_b_p[2],
            _from_feature_major(new_a_t, A_KV_HEADS), _from_feature_major(new_b1_t, B_HEADS),
            _from_feature_major(new_b2_t, B_HEADS), _from_feature_major(new_b3_t, B_HEADS))
```

```python
import functools
import math

import numpy as np
import jax
import jax.numpy as jnp
from jax import lax
from jax.experimental import pallas as pl
from jax.experimental.pallas import tpu as pltpu

F32 = jnp.float32
BF16 = jnp.bfloat16

D_MODEL = 2048
HEAD_DIM = 64
A_WINDOW = 128
A_Q_HEADS = 16
A_KV_HEADS = 2
B_PATTERNS = ((128, 1), (512, 4), (2048, 16))
B_HEADS = 8
NUM_BUCKETS = 32
REL_MAX_DIST = 2048
BLOCK = 128
D_FF = 5632
DEPTH = 1
ALPHA = (2 * DEPTH) ** 0.25
SCALE = HEAD_DIM ** -0.5
LN_EPS = 1e-5
NEG = -1e30
LANES = 128
SUBLANES = 8
DEC_T = 4

A_OUT_W = A_Q_HEADS * HEAD_DIM
A_KV_W = A_KV_HEADS * HEAD_DIM
B_OUT_W = B_HEADS * HEAD_DIM
C_QA, C_KA, C_VA, C_PAD, C_QB, C_KB, C_VB, C_GA, C_GB = 0, 1024, 1152, 1280, 1536, 3072, 4608, 6144, 8192
HW = 10240
QKV_W = C_GA
VMEM_LIMIT = 56 * 1024 * 1024


def _cparams(sem):
    return pltpu.CompilerParams(dimension_semantics=sem, vmem_limit_bytes=VMEM_LIMIT)


def _bucket_np(dist):
    d = np.maximum(np.asarray(dist, np.int64), 0)
    max_exact = NUM_BUCKETS // 2
    ratio = np.maximum(d, max_exact).astype(np.float32) / np.float32(max_exact)
    large = max_exact + (np.log(ratio) / np.float32(math.log(REL_MAX_DIST / max_exact))
                         * np.float32(NUM_BUCKETS - max_exact)).astype(np.int32)
    return np.where(d < max_exact, d, np.minimum(large, NUM_BUCKETS - 1)).astype(np.int32)


def _bias_by_dist(rel_bias, dists, h0, nh):
    return jnp.take(rel_bias[:, h0:h0 + nh], jnp.asarray(_bucket_np(dists)), axis=0).T.astype(F32)


def _band_bias(rel_bias, max_dist, dil, h0, nh):
    u = BLOCK - np.arange(2 * BLOCK)
    w = jnp.where(jnp.asarray((u >= 0) & (u <= max_dist))[None],
                  _bias_by_dist(rel_bias, np.maximum(u, 0) * dil, h0, nh), NEG)
    x = jnp.concatenate([w, jnp.full((nh, 1), NEG, F32)], axis=1)
    band = jnp.tile(x, (1, BLOCK))[:, :BLOCK * 2 * BLOCK].reshape(nh, BLOCK, 2 * BLOCK)
    return jnp.stack([band, jnp.where(jnp.asarray(np.arange(2 * BLOCK) < BLOCK), NEG, band)])


def _packed_rows(per_head, groups, pairs):
    lead, tail = per_head.shape[:-3], per_head.shape[-1]
    t = per_head.reshape(*lead, groups, pairs, 2, BLOCK, tail)
    t = jnp.swapaxes(t, -4, -3)
    return t.reshape(*lead, groups, 2 * pairs * BLOCK, tail)


def _step_bias(rel_bias, win, dil, h0, nh, n_kv, is_a):
    neg = lambda n: jnp.full((nh, n), NEG, F32)
    rows_c = []
    if is_a:
        rev = _bias_by_dist(rel_bias, np.arange(A_WINDOW - 1, -1, -1), h0, nh)
        for t in range(DEC_T):
            rows_c.append(jnp.concatenate([neg(t + 1), rev[:, :win - t - 1]], axis=1))
    elif dil == 1:
        rev = _bias_by_dist(rel_bias, np.arange(win, 0, -1), h0, nh)
        for t in range(DEC_T):
            rows_c.append(jnp.concatenate([neg(t), rev[:, :win - t]], axis=1))
    else:
        rev = _bias_by_dist(rel_bias, np.arange(win // dil, 0, -1) * dil, h0, nh)
        for t in range(DEC_T):
            slots = [rev[:, :, None] if r == t else jnp.full((nh, win // dil, 1), NEG, F32) for r in range(dil)]
            rows_c.append(jnp.concatenate(slots, axis=2).reshape(nh, win))
    bc = jnp.stack(rows_c * 2, axis=1)

    i = np.arange(SUBLANES)[:, None]
    j = np.arange(LANES)[None, :]
    dn = i % DEC_T - j % DEC_T
    vn = (j < SUBLANES) & (i // DEC_T == j // DEC_T) & (dn >= 0)
    if not is_a:
        vn &= dn % dil == 0
    near = _bias_by_dist(rel_bias, np.arange(DEC_T), h0, nh)
    bn = jnp.where(jnp.asarray(vn)[None], near[:, np.clip(dn, 0, DEC_T - 1)], NEG)
    g8 = (nh // n_kv) * SUBLANES
    return bc.reshape(n_kv, g8, win), bn.reshape(n_kv, g8, LANES)


def _inproj_kernel(x_ref, w_ref, o_ref, xb_ref):
    @pl.when(pl.program_id(1) == 0)
    def _():
        xb_ref[...] = x_ref[...].astype(BF16)

    o_ref[...] = jnp.dot(xb_ref[...], w_ref[...], preferred_element_type=F32)


def _in_proj(x, w, tm, tn):
    t, k = x.shape
    n = w.shape[1]
    return pl.pallas_call(
        _inproj_kernel,
        grid=(t // tm, n // tn),
        in_specs=[pl.BlockSpec((tm, k), lambda i, j: (i, 0)),
                  pl.BlockSpec((k, tn), lambda i, j: (0, j))],
        out_specs=pl.BlockSpec((tm, tn), lambda i, j: (i, j)),
        out_shape=jax.ShapeDtypeStruct((t, n), F32),
        scratch_shapes=[pltpu.VMEM((tm, k), BF16)],
        compiler_params=_cparams(("parallel", "arbitrary")),
        name="in_proj",
    )(x, w)


def _gate_proj_kernel(oa_ref, ob_ref, ga_ref, gb_ref, woa_ref, wob_ref, o_ref):
    pa = jnp.dot(oa_ref[...], woa_ref[...], preferred_element_type=F32)
    pb = jnp.dot(ob_ref[...], wob_ref[...], preferred_element_type=F32)
    sa = 0.5 * jnp.tanh(0.5 * ga_ref[...]) + 0.5
    sb = 0.5 * jnp.tanh(0.5 * gb_ref[...]) + 0.5
    o_ref[...] = (sa * pa + sb * pb).astype(o_ref.dtype)


def _gate_proj(oa, ob, h, w_oa, w_ob, tm, tn):
    t = oa.shape[0]
    nj = D_MODEL // tn
    return pl.pallas_call(
        _gate_proj_kernel,
        grid=(t // tm, nj),
        in_specs=[pl.BlockSpec((tm, A_OUT_W), lambda i, j: (i, 0)),
                  pl.BlockSpec((tm, B_OUT_W), lambda i, j: (i, 0)),
                  pl.BlockSpec((tm, tn), lambda i, j: (i, C_GA // tn + j)),
                  pl.BlockSpec((tm, tn), lambda i, j: (i, C_GB // tn + j)),
                  pl.BlockSpec((A_OUT_W, tn), lambda i, j: (0, j)),
                  pl.BlockSpec((B_OUT_W, tn), lambda i, j: (0, j))],
        out_specs=pl.BlockSpec((tm, tn), lambda i, j: (i, j)),
        out_shape=jax.ShapeDtypeStruct((t, D_MODEL), BF16),
        compiler_params=_cparams(("parallel", "arbitrary")),
        name="gate_proj",
    )(oa, ob, h, h, w_oa, w_ob)


def _layer_norm(z, g, b):
    mu = jnp.mean(z, axis=-1, keepdims=True)
    zc = z - mu
    var = jnp.mean(zc * zc, axis=-1, keepdims=True)
    return zc * lax.rsqrt(var + LN_EPS) * g + b


def _mm_res_ln_kernel(a_ref, w_ref, r_ref, g_ref, b_ref, o_ref, *, n_chunk):
    k = pl.program_id(1)
    nk = pl.num_programs(1)
    if n_chunk is None:
        sub = min(128, o_ref.shape[0])
        for r0 in range(0, o_ref.shape[0], sub):
            rows = slice(r0, r0 + sub)
            mix = jnp.dot(a_ref[rows, :], w_ref[...], preferred_element_type=F32)
            o_ref[rows, :] = _layer_norm(ALPHA * r_ref[rows, :] + mix, g_ref[...], b_ref[...])
        return

    @pl.when(k == 0)
    def _():
        o_ref[...] = jnp.zeros_like(o_ref)

    a = a_ref[...]
    for c in range(D_MODEL // n_chunk):
        cols = slice(c * n_chunk, (c + 1) * n_chunk)
        o_ref[:, cols] += jnp.dot(a, w_ref[:, cols], preferred_element_type=F32)

    @pl.when(k == nk - 1)
    def _():
        rows_per_pass = 256
        for r0 in range(0, o_ref.shape[0], rows_per_pass):
            rows = slice(r0, r0 + rows_per_pass)
            o_ref[rows, :] = _layer_norm(ALPHA * r_ref[rows, :] + o_ref[rows, :], g_ref[...], b_ref[...])


def _mm_res_ln(a, w, res, g, b, tm, tk, name):
    t, kdim = a.shape
    return pl.pallas_call(
        functools.partial(_mm_res_ln_kernel, n_chunk=None if tk == kdim else 512),
        grid=(t // tm, kdim // tk),
        in_specs=[pl.BlockSpec((tm, tk), lambda i, k: (i, k)),
                  pl.BlockSpec((tk, D_MODEL), lambda i, k: (k, 0)),
                  pl.BlockSpec((tm, D_MODEL), lambda i, k: (i, 0)),
                  pl.BlockSpec((1, D_MODEL), lambda i, k: (0, 0)),
                  pl.BlockSpec((1, D_MODEL), lambda i, k: (0, 0))],
        out_specs=pl.BlockSpec((tm, D_MODEL), lambda i, k: (i, 0)),
        out_shape=jax.ShapeDtypeStruct((t, D_MODEL), F32),
        compiler_params=_cparams(("parallel", "arbitrary")),
        name=name,
    )(a, w, res, g, b)


def _ffn_in_kernel(h_ref, wg_ref, wu_ref, o_ref, hb_ref):
    @pl.when(pl.program_id(1) == 0)
    def _():
        hb_ref[...] = h_ref[...].astype(BF16)

    hb = hb_ref[...]
    gate = jnp.dot(hb, wg_ref[...], preferred_element_type=F32)
    up = jnp.dot(hb, wu_ref[...], preferred_element_type=F32)
    o_ref[...] = (gate * jax.nn.sigmoid(gate) * up).astype(o_ref.dtype)


def _ffn_in(h1, w_ffn_in, tm, tn):
    t = h1.shape[0]
    nj = D_FF // tn
    return pl.pallas_call(
        _ffn_in_kernel,
        grid=(t // tm, nj),
        in_specs=[pl.BlockSpec((tm, D_MODEL), lambda i, j: (i, 0)),
                  pl.BlockSpec((D_MODEL, tn), lambda i, j: (0, j)),
                  pl.BlockSpec((D_MODEL, tn), lambda i, j: (0, nj + j))],
        out_specs=pl.BlockSpec((tm, tn), lambda i, j: (i, j)),
        out_shape=jax.ShapeDtypeStruct((t, D_FF), BF16),
        scratch_shapes=[pltpu.VMEM((tm, D_MODEL), BF16)],
        compiler_params=_cparams(("parallel", "arbitrary")),
        name="ffn_in",
    )(h1, w_ffn_in, w_ffn_in)


def _head_lane_masks():
    lane = lax.broadcasted_iota(jnp.int32, (1, LANES), 1)
    return [lane < HEAD_DIM, lane >= HEAD_DIM]


def _packed_attention(q, k2b, v2, bias, masks, sinks):
    q2 = jnp.concatenate([jnp.where(mk, q, 0.0) for mk in masks], axis=0).astype(BF16)
    s = lax.dot_general(q2, k2b, (((1,), (1,)), ((), ())), preferred_element_type=F32) + bias
    m = jnp.max(s, axis=-1, keepdims=True)
    if sinks is not None:
        sink = jnp.where(lax.broadcasted_iota(jnp.int32, (2 * BLOCK, 1), 0) < BLOCK, sinks[0], sinks[1])
        m = jnp.maximum(m, sink)
    p = jnp.exp(s - m).astype(BF16)
    nd = jnp.zeros((BLOCK, 2 * LANES), F32)
    for hh, mk in enumerate(masks):
        ones = jnp.broadcast_to(jnp.where(mk, 1.0, 0.0), v2.shape)
        w = jnp.concatenate([jnp.where(mk, v2, 0.0), ones], axis=1).astype(BF16)
        nd = nd + jnp.dot(p[hh * BLOCK:(hh + 1) * BLOCK], w, preferred_element_type=F32)
    den = nd[:, LANES:]
    if sinks is not None:
        es = jnp.exp(sink - m)
        den = den + jnp.where(masks[0], es[:BLOCK], es[BLOCK:])
    return nd[:, :LANES] / den, jnp.where(masks[0], m[:BLOCK], m[BLOCK:]) + jnp.log(den)


def _band_attn_kernel(q_ref, kc_ref, kp_ref, vc_ref, vp_ref, bias_ref, sink_ref, o_ref):
    first = (pl.program_id(1) == 0).astype(jnp.int32)
    masks = _head_lane_masks()
    pairs_per_kv = (A_Q_HEADS // A_KV_HEADS) // 2
    for j in range(q_ref.shape[0] // BLOCK):
        rows = slice(j * BLOCK, (j + 1) * BLOCK)
        if j == 0:
            kp, vp, dead = kp_ref[...], vp_ref[...], first
        else:
            kp, vp, dead = kc_ref[(j - 1) * BLOCK:j * BLOCK, :], vc_ref[(j - 1) * BLOCK:j * BLOCK, :], 0
        kx = jnp.concatenate([kp, kc_ref[rows, :]], axis=0)
        vx = jnp.concatenate([vp, vc_ref[rows, :]], axis=0)
        kr = pltpu.roll(kx, HEAD_DIM, 1)
        vr = pltpu.roll(vx, HEAD_DIM, 1)
        for kv in range(A_KV_HEADS):
            k2 = jnp.where(masks[kv], kx, kr).astype(BF16)
            v2 = jnp.where(masks[kv], vx, vr)
            for pp in range(pairs_per_kv):
                pi = kv * pairs_per_kv + pp
                sl = slice(pi * LANES, (pi + 1) * LANES)
                out, _ = _packed_attention(q_ref[rows, sl], k2, v2, bias_ref[dead, pi], masks,
                                           (sink_ref[2 * pi], sink_ref[2 * pi + 1]))
                o_ref[rows, sl] = out.astype(o_ref.dtype)


def _band_attn_a(h, bias, sink, *, batch, seq, nblk):
    chunk = BLOCK * nblk
    nc = seq // chunk
    qw, kw = A_OUT_W, A_KV_W

    def cur(b, i):
        return b * nc + i

    def prev(b, i):
        return jnp.maximum((b * nc + i) * nblk - 1, 0)

    return pl.pallas_call(
        _band_attn_kernel,
        grid=(batch, nc),
        in_specs=[pl.BlockSpec((chunk, qw), lambda b, i: (cur(b, i), C_QA // qw)),
                  pl.BlockSpec((chunk, kw), lambda b, i: (cur(b, i), C_KA // kw)),
                  pl.BlockSpec((BLOCK, kw), lambda b, i: (prev(b, i), C_KA // kw)),
                  pl.BlockSpec((chunk, kw), lambda b, i: (cur(b, i), C_VA // kw)),
                  pl.BlockSpec((BLOCK, kw), lambda b, i: (prev(b, i), C_VA // kw)),
                  pl.BlockSpec(bias.shape, lambda b, i: (0, 0, 0, 0)),
                  pl.BlockSpec(memory_space=pltpu.SMEM)],
        out_specs=pl.BlockSpec((chunk, qw), lambda b, i: (cur(b, i), 0)),
        out_shape=jax.ShapeDtypeStruct((batch * seq, qw), BF16),
        compiler_params=_cparams(("parallel", "arbitrary")),
        name="band_attn_a",
    )(h, h, h, h, h, bias, sink)


def _dil_attn_kernel(*refs, dil, nblk, unroll, has_prev, n_merge):
    refs = list(refs)
    q_ref, kc_ref, vc_ref = refs[:3]
    del refs[:3]
    if has_prev:
        kp_ref, vp_ref = refs[:2]
        del refs[:2]
    bias_ref = refs.pop(0)
    others = []
    if n_merge:
        others = [(refs[2 * g], refs[2 * g + 1]) for g in range(n_merge)]
        del refs[:2 * n_merge]
        (o_ref,) = refs
    else:
        o_ref, lse_ref = refs
    first = (pl.program_id(2) == 0).astype(jnp.int32)
    masks = _head_lane_masks()

    def rows(r, j):
        return pl.ds(j * BLOCK * dil + r, BLOCK, stride=dil) if dil > 1 else pl.ds(j * BLOCK, BLOCK)

    def tile(r, j):
        cur = rows(r, j)
        if not has_prev:
            k2, v2, bias = kc_ref[cur, :].astype(BF16), vc_ref[cur, :], bias_ref[0, 0][:, BLOCK:]
        else:
            if j == 0:
                kp, vp, dead = kp_ref[rows(r, 0), :], vp_ref[rows(r, 0), :], first
            else:
                kp, vp, dead = kc_ref[rows(r, j - 1), :], vc_ref[rows(r, j - 1), :], 0
            k2 = jnp.concatenate([kp, kc_ref[cur, :]], axis=0).astype(BF16)
            v2 = jnp.concatenate([vp, vc_ref[cur, :]], axis=0)
            bias = bias_ref[dead, 0]
        out, lse = _packed_attention(q_ref[cur, :], k2, v2, bias, masks, None)
        if not n_merge:
            o_ref[cur, :] = out
            lse_ref[cur, :] = lse
            return
        lses = [lse] + [l_ref[cur, :] for _, l_ref in others]
        outs = [out] + [og_ref[cur, :] for og_ref, _ in others]
        mx = functools.reduce(jnp.maximum, lses)
        es = [jnp.exp(l - mx) for l in lses]
        o_ref[cur, :] = (sum(e * o for e, o in zip(es, outs)) / sum(es)).astype(o_ref.dtype)

    def body(it, carry):
        for u in range(unroll):
            for j in range(nblk):
                tile(it * unroll + u, j)
        return carry

    if dil == unroll:
        body(0, 0)
    else:
        lax.fori_loop(0, dil // unroll, body, 0)


def _dil_attn(h, bias, *, batch, seq, dil, nblk, cq, ck, cv, merge=()):
    band = BLOCK * dil
    chunk = band * nblk
    nc = seq // chunk
    pairs = B_OUT_W // LANES

    def cur(b, p, i):
        return b * nc + i

    def prev(b, p, i):
        return jnp.maximum((b * nc + i) * nblk - 1, 0)

    o_spec = pl.BlockSpec((chunk, LANES), lambda b, p, i: (cur(b, p, i), p))
    o_shape = jax.ShapeDtypeStruct((batch * seq, B_OUT_W), F32)
    has_prev = seq > band
    in_specs = [pl.BlockSpec((chunk, LANES), lambda b, p, i: (cur(b, p, i), cq // LANES + p)),
                pl.BlockSpec((chunk, LANES), lambda b, p, i: (cur(b, p, i), ck // LANES + p)),
                pl.BlockSpec((chunk, LANES), lambda b, p, i: (cur(b, p, i), cv // LANES + p))]
    if has_prev:
        in_specs += [pl.BlockSpec((band, LANES), lambda b, p, i: (prev(b, p, i), ck // LANES + p)),
                     pl.BlockSpec((band, LANES), lambda b, p, i: (prev(b, p, i), cv // LANES + p))]
    n_h = len(in_specs)
    in_specs.append(pl.BlockSpec((2, 1, 2 * BLOCK, 2 * BLOCK), lambda b, p, i: (0, p, 0, 0)))
    others = [a for pair in merge for a in pair]
    in_specs += [o_spec] * len(others)
    if merge:
        assert dil == 1, "the bf16 output needs unstrided stores"
        out_specs, out_shape = o_spec, jax.ShapeDtypeStruct((batch * seq, B_OUT_W), BF16)
    else:
        out_specs, out_shape = (o_spec, o_spec), (o_shape, o_shape)
    return pl.pallas_call(
        functools.partial(_dil_attn_kernel, dil=dil, nblk=nblk, unroll=min(dil, 8 // nblk), has_prev=has_prev,
                          n_merge=len(merge)),
        grid=(batch, pairs, nc),
        in_specs=in_specs,
        out_specs=out_specs,
        out_shape=out_shape,
        compiler_params=_cparams(("parallel", "parallel", "arbitrary")),
        name=f"dil_attn_d{dil}",
    )(*([h] * n_h), bias, *others)


def _kv_tail_kernel(k_ref, v_ref, o_ref):
    cw = k_ref.shape[1]
    o_ref[0, 0:cw, :] = k_ref[...].T
    o_ref[0, cw:2 * cw, :] = v_ref[...].T


def _kv_tail(h, *, batch, seq, win, cw, ck, cv):
    rows = min(win, 4 * BLOCK)
    nblk = win // rows
    base = (seq - win) // rows
    per = seq // rows
    return pl.pallas_call(
        _kv_tail_kernel,
        grid=(batch, nblk),
        in_specs=[pl.BlockSpec((rows, cw), lambda b, i: (b * per + base + i, ck // cw)),
                  pl.BlockSpec((rows, cw), lambda b, i: (b * per + base + i, cv // cw))],
        out_specs=pl.BlockSpec((1, 2 * cw, rows), lambda b, i: (b, 0, i)),
        out_shape=jax.ShapeDtypeStruct((batch, 2 * cw, win), F32),
        compiler_params=_cparams(("parallel", "parallel")),
        name=f"kv_tail_w{win}_c{cw}",
    )(h, h)


def _shift_window(c_ref, cout_ref, knf, vnf, lo):
    lane = lax.broadcasted_iota(jnp.int32, (1, LANES), 1)
    keep = lane < LANES - DEC_T
    to_tail = lax.rem(2 * LANES - DEC_T - lo, LANES)
    new_tail = jnp.concatenate([pltpu.roll(knf, to_tail, 1), pltpu.roll(vnf, to_tail, 1)], axis=0)
    nlb = c_ref.shape[2] // LANES
    nxt = pltpu.roll(c_ref[0, :, 0:LANES], LANES - DEC_T, 1)
    for j in range(nlb):
        cur = nxt
        if j + 1 < nlb:
            nxt = pltpu.roll(c_ref[0, :, (j + 1) * LANES:(j + 2) * LANES], LANES - DEC_T, 1)
        else:
            nxt = new_tail
        cout_ref[0, :, j * LANES:(j + 1) * LANES] = jnp.where(keep, cur, nxt)


def _pad_rows(x):
    return jnp.concatenate([x, jnp.zeros((LANES - SUBLANES, x.shape[1]), F32)], axis=0).astype(BF16)


def _step_attend_a(c_ref, q, knt, vnt, bc_ref, bn_ref, sink_ref, o_scr):
    kvw = A_KV_W
    group = A_Q_HEADS // A_KV_HEADS
    for kv in range(A_KV_HEADS):
        ksl = slice(kv * HEAD_DIM, (kv + 1) * HEAD_DIM)
        kt = c_ref[0, ksl, :].astype(BF16)
        vt = c_ref[0, kvw + kv * HEAD_DIM:kvw + (kv + 1) * HEAD_DIM, :].astype(BF16)
        heads = range(kv * group, (kv + 1) * group)
        qs = jnp.concatenate([q[:, h * HEAD_DIM:(h + 1) * HEAD_DIM] for h in heads], axis=0).astype(BF16)
        s = jnp.dot(qs, kt, preferred_element_type=F32) + bc_ref[kv]
        sn = lax.dot_general(qs, _pad_rows(knt[:, ksl]), (((1,), (1,)), ((), ())), preferred_element_type=F32) + bn_ref[kv]
        sink = sink_ref[kv][:, 0:1]
        m = jnp.maximum(jnp.maximum(jnp.max(s, axis=-1, keepdims=True), jnp.max(sn, axis=-1, keepdims=True)), sink)
        p = jnp.exp(s - m)
        pn = jnp.exp(sn - m)
        l = jnp.sum(p, axis=-1, keepdims=True) + jnp.sum(pn, axis=-1, keepdims=True) + jnp.exp(sink - m)
        o = lax.dot_general(p.astype(BF16), vt, (((1,), (1,)), ((), ())), preferred_element_type=F32)
        o = (o + jnp.dot(pn.astype(BF16), _pad_rows(vnt[:, ksl]), preferred_element_type=F32)) / l
        for g, h in enumerate(heads):
            o_scr[:, h * HEAD_DIM:(h + 1) * HEAD_DIM] = o[g * SUBLANES:(g + 1) * SUBLANES, :]


def _step_attend_b(c_ref, q, knt, vnt, bc_ref, bn_ref, head_lanes):
    kw = B_OUT_W
    qx = jnp.concatenate([jnp.where(mk, q, 0.0) for mk in head_lanes], axis=0).astype(BF16)
    kt = c_ref[0, 0:kw, :].astype(BF16)
    vt = c_ref[0, kw:2 * kw, :].astype(BF16)
    s = jnp.dot(qx, kt, preferred_element_type=F32) + bc_ref[...]
    sn = lax.dot_general(qx, _pad_rows(knt), (((1,), (1,)), ((), ())), preferred_element_type=F32) + bn_ref[...]
    m = jnp.maximum(jnp.max(s, axis=-1, keepdims=True), jnp.max(sn, axis=-1, keepdims=True))
    p = jnp.exp(s - m)
    pn = jnp.exp(sn - m)
    l = jnp.sum(p, axis=-1, keepdims=True) + jnp.sum(pn, axis=-1, keepdims=True)
    ox = lax.dot_general(p.astype(BF16), vt, (((1,), (1,)), ((), ())), preferred_element_type=F32)
    ox = ox + jnp.dot(pn.astype(BF16), _pad_rows(vnt), preferred_element_type=F32)
    o = jnp.zeros((SUBLANES, kw), F32)
    lrow = jnp.ones((SUBLANES, kw), F32)
    mrow = jnp.zeros((SUBLANES, kw), F32)
    for h, mk in enumerate(head_lanes):
        rows = slice(h * SUBLANES, (h + 1) * SUBLANES)
        o = jnp.where(mk, ox[rows], o)
        lrow = jnp.where(mk, l[rows], lrow)
        mrow = jnp.where(mk, m[rows], mrow)
    return o / lrow, mrow + jnp.log(lrow)


def _step_kernel(ca_ref, c1_ref, c2_ref, c3_ref, hq_ref, fa_ref, fb_ref,
                 bca_ref, bna_ref, sink_ref, bc1_ref, bn1_ref, bc2_ref, bn2_ref, bc3_ref, bn3_ref,
                 na_ref, n1_ref, n2_ref, n3_ref, oa_ref, ob_ref, oa_scr):
    b = pl.program_id(0)
    lo = DEC_T * lax.rem(b, LANES // DEC_T)

    _shift_window(ca_ref, na_ref, fa_ref[0:A_KV_W, :], fa_ref[A_KV_W:2 * A_KV_W, :], lo)
    groups = ((c1_ref, n1_ref, bc1_ref, bn1_ref), (c2_ref, n2_ref, bc2_ref, bn2_ref), (c3_ref, n3_ref, bc3_ref, bn3_ref))
    n_groups = len(groups)
    for g, (c_ref, n_ref, _, _) in enumerate(groups):
        _shift_window(c_ref, n_ref, fb_ref[g * B_OUT_W:(g + 1) * B_OUT_W, :],
                      fb_ref[(n_groups + g) * B_OUT_W:(n_groups + g + 1) * B_OUT_W, :], lo)

    _step_attend_a(ca_ref, hq_ref[:, C_QA:C_QA + A_OUT_W], hq_ref[:, C_KA:C_KA + A_KV_W], hq_ref[:, C_VA:C_VA + A_KV_W],
                   bca_ref, bna_ref, sink_ref, oa_scr)

    lane = lax.broadcasted_iota(jnp.int32, (1, B_OUT_W), 1)
    head_lanes = [lane // HEAD_DIM == h for h in range(B_HEADS)]
    outs, lses = [], []
    for g, (c_ref, _, bc_ref, bn_ref) in enumerate(groups):
        sl = lambda c0: slice(c0 + g * B_OUT_W, c0 + (g + 1) * B_OUT_W)
        o, lse = _step_attend_b(c_ref, hq_ref[:, sl(C_QB)], hq_ref[:, sl(C_KB)], hq_ref[:, sl(C_VB)], bc_ref, bn_ref,
                                head_lanes)
        outs.append(o)
        lses.append(lse)
    mx = functools.reduce(jnp.maximum, lses)
    es = [jnp.exp(lse - mx) for lse in lses]
    ob = sum(e * o for e, o in zip(es, outs)) / sum(es)

    half = lax.rem(b, 2)
    mine = lax.broadcasted_iota(jnp.int32, (SUBLANES, 1), 0) // DEC_T == half

    @pl.when(half == 0)
    def _():
        oa_ref[...] = jnp.where(mine, oa_scr[...], 0.0)
        ob_ref[...] = jnp.where(mine, ob, 0.0)

    @pl.when(half == 1)
    def _():
        oa_ref[...] = jnp.where(mine, oa_scr[...], oa_ref[...])
        ob_ref[...] = jnp.where(mine, ob, ob_ref[...])


def _step_attn(caches, hs, ht, tables):
    db = caches[0].shape[0]
    per_tile = LANES // DEC_T
    per_blk = SUBLANES // DEC_T
    n_kb = len(B_PATTERNS) * B_OUT_W
    cache_specs = [pl.BlockSpec((1,) + c.shape[1:], lambda b: (b, 0, 0)) for c in caches]
    in_specs = cache_specs + [
        pl.BlockSpec((SUBLANES, QKV_W), lambda b: (b // per_blk, 0)),
        pl.BlockSpec((2 * A_KV_W, LANES), lambda b: (C_KA // (2 * A_KV_W), b // per_tile)),
        pl.BlockSpec((2 * n_kb, LANES), lambda b: (C_KB // (2 * n_kb), b // per_tile)),
    ] + [pl.BlockSpec(t.shape, lambda b, nd=t.ndim: (0,) * nd) for t in tables]
    o_specs = [pl.BlockSpec((SUBLANES, A_OUT_W), lambda b: (b // per_blk, 0)),
               pl.BlockSpec((SUBLANES, B_OUT_W), lambda b: (b // per_blk, 0))]
    return pl.pallas_call(
        _step_kernel,
        grid=(db,),
        in_specs=in_specs,
        out_specs=tuple(cache_specs + o_specs),
        out_shape=tuple([jax.ShapeDtypeStruct(c.shape, F32) for c in caches]
                        + [jax.ShapeDtypeStruct((db * DEC_T, A_OUT_W), F32), jax.ShapeDtypeStruct((db * DEC_T, B_OUT_W), F32)]),
        scratch_shapes=[pltpu.VMEM((SUBLANES, A_OUT_W), F32)],
        compiler_params=_cparams(("arbitrary",)),
        name="step_attn",
    )(*caches, hs, ht, ht, *tables)


PREP_W = C_QB - C_PAD


def _prep_w_in_kernel(w_ref, o_ref):
    j = pl.program_id(0)
    is_q = (j < C_KA // PREP_W) | ((j >= C_QB // PREP_W) & (j < C_KB // PREP_W))
    scale = jnp.where(is_q, SCALE, 1.0)
    o_ref[...] = jnp.where(j == C_PAD // PREP_W, 0.0, w_ref[...] * scale).astype(o_ref.dtype)


def _prep_w_in(w):
    pad_blk = C_PAD // PREP_W
    return pl.pallas_call(
        _prep_w_in_kernel,
        grid=(HW // PREP_W,),
        in_specs=[pl.BlockSpec((w.shape[0], PREP_W), lambda j: (0, jnp.where(j > pad_blk, j - 1, jnp.minimum(j, pad_blk - 1))))],
        out_specs=pl.BlockSpec((w.shape[0], PREP_W), lambda j: (0, j)),
        out_shape=jax.ShapeDtypeStruct((w.shape[0], HW), BF16),
        compiler_params=_cparams(("parallel",)),
        name="prep_w_in",
    )(w)


def _to_feature_major(cache):
    db, win = cache.shape[:2]
    return jnp.transpose(cache, (0, 2, 3, 4, 1)).reshape(db, -1, win)


def _from_feature_major(ct, heads):
    n, _, win = ct.shape
    return jnp.transpose(ct.reshape(n, 2, heads, HEAD_DIM, win), (0, 4, 1, 2, 3))[None]


def _tail_layers(x, h, oa, ob, w_oa, w_ob, w_out, ln1_g, ln1_b, w_ffn_in, w_ffn_out, ln2_g, ln2_b):
    t = x.shape[0]
    tm = min(1024, t)
    mixin = _gate_proj(oa, ob, h, w_oa, w_ob, tm, 1024)
    h1 = _mm_res_ln(mixin, w_out, x, ln1_g, ln1_b, min(512, t), D_MODEL, "out_ln1")
    u = _ffn_in(h1, w_ffn_in, tm, 512)
    return _mm_res_ln(u, w_ffn_out, h1, ln2_g, ln2_b, tm, D_FF // 4, "ffn_out_ln2")


def kernel(x_prompt, x_sample, cache_a, cache_b1, cache_b2, cache_b3, rel_bias, w_in, a_sink, w_oa, w_ob,
           w_out, ln1_g, ln1_b, w_ffn_in, w_ffn_out, ln2_g, ln2_b):
    batch, seq, _ = x_prompt.shape
    db, dt, _ = x_sample.shape
    assert dt == DEC_T and w_in.shape[0] == DEPTH and db % (SUBLANES // DEC_T) == 0
    tp, ts = batch * seq, db * dt

    w_in_b = _prep_w_in(w_in[0])
    weights = (w_oa[0].astype(BF16), w_ob[0].astype(BF16), w_out[0].astype(BF16), ln1_g, ln1_b,
               w_ffn_in[0].astype(BF16), w_ffn_out[0].astype(BF16), ln2_g, ln2_b)
    sink = a_sink[0].astype(F32)
    b_h0 = [A_Q_HEADS + g * B_HEADS for g in range(len(B_PATTERNS))]
    b_cols = [(C_QB + g * B_OUT_W, C_KB + g * B_OUT_W, C_VB + g * B_OUT_W) for g in range(len(B_PATTERNS))]

    xp = x_prompt.reshape(tp, D_MODEL)
    hp = _in_proj(xp, w_in_b, 1024, 2048)
    bias_a = _packed_rows(_band_bias(rel_bias, A_WINDOW - 1, 1, 0, A_Q_HEADS), A_Q_HEADS // 2, 1)
    oa = _band_attn_a(hp, bias_a, sink, batch=batch, seq=seq, nblk=2)
    others, ob = [], None
    for g, (win, dil) in sorted(enumerate(B_PATTERNS), key=lambda e: -e[1][1]):
        cq, ck, cv = b_cols[g]
        bias_g = _packed_rows(_band_bias(rel_bias, win // dil, dil, b_h0[g], B_HEADS), B_HEADS // 2, 1)
        res = _dil_attn(hp, bias_g, batch=batch, seq=seq, dil=dil, nblk={1: 8, 4: 2, 16: 1}[dil], cq=cq, ck=ck, cv=cv,
                        merge=others if dil == 1 else ())
        if dil == 1:
            ob = res
        else:
            others.append(res)
    yp = _tail_layers(xp, hp, oa, ob, *weights).reshape(batch, seq, D_MODEL)

    new_a_p = _from_feature_major(_kv_tail(hp, batch=batch, seq=seq, win=min(A_WINDOW, seq), cw=A_KV_W, ck=C_KA, cv=C_VA),
                                  A_KV_HEADS)
    new_b_p = [_from_feature_major(_kv_tail(hp, batch=batch, seq=seq, win=min(win, seq), cw=B_OUT_W,
                                            ck=b_cols[g][1], cv=b_cols[g][2]), B_HEADS)
               for g, (win, dil) in enumerate(B_PATTERNS)]

    xs = x_sample.reshape(ts, D_MODEL)
    hs = _in_proj(xs, w_in_b, ts, 1024)
    lanes = -(-ts // LANES) * LANES
    ht = jnp.pad(hs[:, :QKV_W].T, ((0, 0), (0, lanes - ts)))

    group_a = A_Q_HEADS // A_KV_HEADS
    sink_rows = jnp.broadcast_to(jnp.repeat(sink.reshape(A_KV_HEADS, group_a), SUBLANES, axis=1)[:, :, None],
                                 (A_KV_HEADS, group_a * SUBLANES, LANES))
    tables = list(_step_bias(rel_bias, cache_a.shape[2], 1, 0, A_Q_HEADS, A_KV_HEADS, True)) + [sink_rows]
    for g, ((win, dil), cache) in enumerate(zip(B_PATTERNS, (cache_b1, cache_b2, cache_b3))):
        bc, bn = _step_bias(rel_bias, cache.shape[2], dil, b_h0[g], B_HEADS, 1, False)
        tables += [bc[0], bn[0]]
    caches = [_to_feature_major(c[0]) for c in (cache_a, cache_b1, cache_b2, cache_b3)]
    new_a_t, new_b1_t, new_b2_t, new_b3_t, oa_s, ob_s = _step_attn(caches, hs, ht, tables)
    ys = _tail_layers(xs, hs, oa_s.astype(BF16), ob_s.astype(BF16), *weights).reshape(db, dt, D_MODEL)

    return (yp, ys, new_a_p, new_b_p[0], new_b_p[1], new_b_p[2],
            _from_feature_major(new_a_t, A_KV_HEADS), _from_feature_major(new_b1_t, B_HEADS),
            _from_feature_major(new_b2_t, B_HEADS), _from_feature_major(new_b3_t, B_HEADS))
```

```python
import functools
import math

import numpy as np
import jax
import jax.numpy as jnp
from jax import lax
from jax.experimental import pallas as pl
from jax.experimental.pallas import tpu as pltpu

F32 = jnp.float32
BF16 = jnp.bfloat16

D_MODEL = 2048
HEAD_DIM = 64
A_WINDOW = 128
A_Q_HEADS = 16
A_KV_HEADS = 2
B_PATTERNS = ((128, 1), (512, 4), (2048, 16))
B_HEADS = 8
NUM_BUCKETS = 32
REL_MAX_DIST = 2048
BLOCK = 128
D_FF = 5632
DEPTH = 1
ALPHA = (2 * DEPTH) ** 0.25
SCALE = HEAD_DIM ** -0.5
LN_EPS = 1e-5
NEG = -1e30
LANES = 128
SUBLANES = 8
DEC_T = 4

A_OUT_W = A_Q_HEADS * HEAD_DIM
A_KV_W = A_KV_HEADS * HEAD_DIM
B_OUT_W = B_HEADS * HEAD_DIM
C_QA, C_KA, C_VA, C_PAD, C_QB, C_KB, C_VB, C_GA, C_GB = 0, 1024, 1152, 1280, 1536, 3072, 4608, 6144, 8192
HW = 10240
QKV_W = C_GA
VMEM_LIMIT = 56 * 1024 * 1024


def _cparams(sem):
    return pltpu.CompilerParams(dimension_semantics=sem, vmem_limit_bytes=VMEM_LIMIT)


def _bucket_np(dist):
    d = np.maximum(np.asarray(dist, np.int64), 0)
    max_exact = NUM_BUCKETS // 2
    ratio = np.maximum(d, max_exact).astype(np.float32) / np.float32(max_exact)
    large = max_exact + (np.log(ratio) / np.float32(math.log(REL_MAX_DIST / max_exact))
                         * np.float32(NUM_BUCKETS - max_exact)).astype(np.int32)
    return np.where(d < max_exact, d, np.minimum(large, NUM_BUCKETS - 1)).astype(np.int32)


def _bias_by_dist(rel_bias, dists, h0, nh):
    return jnp.take(rel_bias[:, h0:h0 + nh], jnp.asarray(_bucket_np(dists)), axis=0).T.astype(F32)


def _band_bias(rel_bias, max_dist, dil, h0, nh):
    u = BLOCK - np.arange(2 * BLOCK)
    w = jnp.where(jnp.asarray((u >= 0) & (u <= max_dist))[None],
                  _bias_by_dist(rel_bias, np.maximum(u, 0) * dil, h0, nh), NEG)
    x = jnp.concatenate([w, jnp.full((nh, 1), NEG, F32)], axis=1)
    band = jnp.tile(x, (1, BLOCK))[:, :BLOCK * 2 * BLOCK].reshape(nh, BLOCK, 2 * BLOCK)
    return jnp.stack([band, jnp.where(jnp.asarray(np.arange(2 * BLOCK) < BLOCK), NEG, band)])


def _packed_rows(per_head, groups, pairs):
    lead, tail = per_head.shape[:-3], per_head.shape[-1]
    t = per_head.reshape(*lead, groups, pairs, 2, BLOCK, tail)
    t = jnp.swapaxes(t, -4, -3)
    return t.reshape(*lead, groups, 2 * pairs * BLOCK, tail)


def _step_bias(rel_bias, win, dil, h0, nh, n_kv, is_a):
    neg = lambda n: jnp.full((nh, n), NEG, F32)
    rows_c = []
    if is_a:
        rev = _bias_by_dist(rel_bias, np.arange(A_WINDOW - 1, -1, -1), h0, nh)
        for t in range(DEC_T):
            rows_c.append(jnp.concatenate([neg(t + 1), rev[:, :win - t - 1]], axis=1))
    elif dil == 1:
        rev = _bias_by_dist(rel_bias, np.arange(win, 0, -1), h0, nh)
        for t in range(DEC_T):
            rows_c.append(jnp.concatenate([neg(t), rev[:, :win - t]], axis=1))
    else:
        rev = _bias_by_dist(rel_bias, np.arange(win // dil, 0, -1) * dil, h0, nh)
        for t in range(DEC_T):
            slots = [rev[:, :, None] if r == t else jnp.full((nh, win // dil, 1), NEG, F32) for r in range(dil)]
            rows_c.append(jnp.concatenate(slots, axis=2).reshape(nh, win))
    bc = jnp.stack(rows_c * 2, axis=1)

    i = np.arange(SUBLANES)[:, None]
    j = np.arange(LANES)[None, :]
    dn = i % DEC_T - j % DEC_T
    vn = (j < SUBLANES) & (i // DEC_T == j // DEC_T) & (dn >= 0)
    if not is_a:
        vn &= dn % dil == 0
    near = _bias_by_dist(rel_bias, np.arange(DEC_T), h0, nh)
    bn = jnp.where(jnp.asarray(vn)[None], near[:, np.clip(dn, 0, DEC_T - 1)], NEG)
    g8 = (nh // n_kv) * SUBLANES
    return bc.reshape(n_kv, g8, win), bn.reshape(n_kv, g8, LANES)


def _inproj_kernel(x_ref, w_ref, o_ref, xb_ref):
    @pl.when(pl.program_id(1) == 0)
    def _():
        xb_ref[...] = x_ref[...].astype(BF16)

    o_ref[...] = jnp.dot(xb_ref[...], w_ref[...], preferred_element_type=F32)


def _in_proj(x, w, tm, tn):
    t, k = x.shape
    n = w.shape[1]
    return pl.pallas_call(
        _inproj_kernel,
        grid=(t // tm, n // tn),
        in_specs=[pl.BlockSpec((tm, k), lambda i, j: (i, 0)),
                  pl.BlockSpec((k, tn), lambda i, j: (0, j))],
        out_specs=pl.BlockSpec((tm, tn), lambda i, j: (i, j)),
        out_shape=jax.ShapeDtypeStruct((t, n), F32),
        scratch_shapes=[pltpu.VMEM((tm, k), BF16)],
        compiler_params=_cparams(("parallel", "arbitrary")),
        name="in_proj",
    )(x, w)


def _gate_proj_kernel(oa_ref, ob_ref, ga_ref, gb_ref, woa_ref, wob_ref, o_ref):
    pa = jnp.dot(oa_ref[...], woa_ref[...], preferred_element_type=F32)
    pb = jnp.dot(ob_ref[...], wob_ref[...], preferred_element_type=F32)
    sa = 0.5 * jnp.tanh(0.5 * ga_ref[...]) + 0.5
    sb = 0.5 * jnp.tanh(0.5 * gb_ref[...]) + 0.5
    o_ref[...] = (sa * pa + sb * pb).astype(o_ref.dtype)


def _gate_proj(oa, ob, h, w_oa, w_ob, tm, tn):
    t = oa.shape[0]
    nj = D_MODEL // tn
    return pl.pallas_call(
        _gate_proj_kernel,
        grid=(t // tm, nj),
        in_specs=[pl.BlockSpec((tm, A_OUT_W), lambda i, j: (i, 0)),
                  pl.BlockSpec((tm, B_OUT_W), lambda i, j: (i, 0)),
                  pl.BlockSpec((tm, tn), lambda i, j: (i, C_GA // tn + j)),
                  pl.BlockSpec((tm, tn), lambda i, j: (i, C_GB // tn + j)),
                  pl.BlockSpec((A_OUT_W, tn), lambda i, j: (0, j)),
                  pl.BlockSpec((B_OUT_W, tn), lambda i, j: (0, j))],
        out_specs=pl.BlockSpec((tm, tn), lambda i, j: (i, j)),
        out_shape=jax.ShapeDtypeStruct((t, D_MODEL), BF16),
        compiler_params=_cparams(("parallel", "arbitrary")),
        name="gate_proj",
    )(oa, ob, h, h, w_oa, w_ob)


def _layer_norm(z, g, b):
    mu = jnp.mean(z, axis=-1, keepdims=True)
    zc = z - mu
    var = jnp.mean(zc * zc, axis=-1, keepdims=True)
    return zc * lax.rsqrt(var + LN_EPS) * g + b


def _mm_res_ln_kernel(a_ref, w_ref, r_ref, g_ref, b_ref, o_ref, ob_ref=None, *, n_chunk):
    k = pl.program_id(1)
    nk = pl.num_programs(1)
    if n_chunk is None:
        sub = min(128, o_ref.shape[0])
        for r0 in range(0, o_ref.shape[0], sub):
            rows = slice(r0, r0 + sub)
            mix = jnp.dot(a_ref[rows, :], w_ref[...], preferred_element_type=F32)
            y = _layer_norm(ALPHA * r_ref[rows, :] + mix, g_ref[...], b_ref[...])
            o_ref[rows, :] = y
            if ob_ref is not None:
                ob_ref[rows, :] = y.astype(ob_ref.dtype)
        return

    @pl.when(k == 0)
    def _():
        o_ref[...] = jnp.zeros_like(o_ref)

    a = a_ref[...]
    for c in range(D_MODEL // n_chunk):
        cols = slice(c * n_chunk, (c + 1) * n_chunk)
        o_ref[:, cols] += jnp.dot(a, w_ref[:, cols], preferred_element_type=F32)

    @pl.when(k == nk - 1)
    def _():
        rows_per_pass = 256
        for r0 in range(0, o_ref.shape[0], rows_per_pass):
            rows = slice(r0, r0 + rows_per_pass)
            o_ref[rows, :] = _layer_norm(ALPHA * r_ref[rows, :] + o_ref[rows, :], g_ref[...], b_ref[...])


def _mm_res_ln(a, w, res, g, b, tm, tk, name, also_bf16=False):
    t, kdim = a.shape
    assert not also_bf16 or tk == kdim
    o_spec = pl.BlockSpec((tm, D_MODEL), lambda i, k: (i, 0))
    o_shape = jax.ShapeDtypeStruct((t, D_MODEL), F32)
    return pl.pallas_call(
        functools.partial(_mm_res_ln_kernel, n_chunk=None if tk == kdim else 512),
        grid=(t // tm, kdim // tk),
        in_specs=[pl.BlockSpec((tm, tk), lambda i, k: (i, k)),
                  pl.BlockSpec((tk, D_MODEL), lambda i, k: (k, 0)),
                  pl.BlockSpec((tm, D_MODEL), lambda i, k: (i, 0)),
                  pl.BlockSpec((1, D_MODEL), lambda i, k: (0, 0)),
                  pl.BlockSpec((1, D_MODEL), lambda i, k: (0, 0))],
        out_specs=(o_spec, o_spec) if also_bf16 else o_spec,
        out_shape=(o_shape, jax.ShapeDtypeStruct((t, D_MODEL), BF16)) if also_bf16 else o_shape,
        compiler_params=_cparams(("parallel", "arbitrary")),
        name=name,
    )(a, w, res, g, b)


def _ffn_in_kernel(h_ref, wg_ref, wu_ref, o_ref, *, sub):
    wg = wg_ref[...].astype(BF16)
    wu = wu_ref[...].astype(BF16)
    for r0 in range(0, o_ref.shape[0], sub):
        rows = slice(r0, r0 + sub)
        hb = h_ref[rows, :]
        gate = jnp.dot(hb, wg, preferred_element_type=F32)
        up = jnp.dot(hb, wu, preferred_element_type=F32)
        o_ref[rows, :] = (gate * jax.nn.sigmoid(gate) * up).astype(o_ref.dtype)


def _ffn_in(h1b, w_ffn_in, tm, tn):
    t = h1b.shape[0]
    nj = D_FF // tn
    return pl.pallas_call(
        functools.partial(_ffn_in_kernel, sub=min(1024, tm)),
        grid=(t // tm, nj),
        in_specs=[pl.BlockSpec((tm, D_MODEL), lambda i, j: (i, 0)),
                  pl.BlockSpec((D_MODEL, tn), lambda i, j: (0, j)),
                  pl.BlockSpec((D_MODEL, tn), lambda i, j: (0, nj + j))],
        out_specs=pl.BlockSpec((tm, tn), lambda i, j: (i, j)),
        out_shape=jax.ShapeDtypeStruct((t, D_FF), BF16),
        compiler_params=_cparams(("parallel", "arbitrary")),
        name="ffn_in",
    )(h1b, w_ffn_in, w_ffn_in)


def _head_lane_masks():
    lane = lax.broadcasted_iota(jnp.int32, (1, LANES), 1)
    return [lane < HEAD_DIM, lane >= HEAD_DIM]


def _packed_attention(q, k2b, v2, bias, masks, sinks):
    q2 = jnp.concatenate([jnp.where(mk, q, 0.0) for mk in masks], axis=0).astype(BF16)
    s = lax.dot_general(q2, k2b, (((1,), (1,)), ((), ())), preferred_element_type=F32) + bias
    m = jnp.max(s, axis=-1, keepdims=True)
    if sinks is not None:
        sink = jnp.where(lax.broadcasted_iota(jnp.int32, (2 * BLOCK, 1), 0) < BLOCK, sinks[0], sinks[1])
        m = jnp.maximum(m, sink)
    p = jnp.exp(s - m).astype(BF16)
    nd = jnp.zeros((BLOCK, 2 * LANES), F32)
    for hh, mk in enumerate(masks):
        ones = jnp.broadcast_to(jnp.where(mk, 1.0, 0.0), v2.shape)
        w = jnp.concatenate([jnp.where(mk, v2, 0.0), ones], axis=1).astype(BF16)
        nd = nd + jnp.dot(p[hh * BLOCK:(hh + 1) * BLOCK], w, preferred_element_type=F32)
    den = nd[:, LANES:]
    if sinks is not None:
        es = jnp.exp(sink - m)
        den = den + jnp.where(masks[0], es[:BLOCK], es[BLOCK:])
    return nd[:, :LANES] / den, jnp.where(masks[0], m[:BLOCK], m[BLOCK:]) + jnp.log(den)


def _band_attn_kernel(q_ref, kc_ref, kp_ref, vc_ref, vp_ref, bias_ref, sink_ref, o_ref):
    first = (pl.program_id(1) == 0).astype(jnp.int32)
    masks = _head_lane_masks()
    pairs_per_kv = (A_Q_HEADS // A_KV_HEADS) // 2
    for j in range(q_ref.shape[0] // BLOCK):
        rows = slice(j * BLOCK, (j + 1) * BLOCK)
        if j == 0:
            kp, vp, dead = kp_ref[...], vp_ref[...], first
        else:
            kp, vp, dead = kc_ref[(j - 1) * BLOCK:j * BLOCK, :], vc_ref[(j - 1) * BLOCK:j * BLOCK, :], 0
        kx = jnp.concatenate([kp, kc_ref[rows, :]], axis=0)
        vx = jnp.concatenate([vp, vc_ref[rows, :]], axis=0)
        kr = pltpu.roll(kx, HEAD_DIM, 1)
        vr = pltpu.roll(vx, HEAD_DIM, 1)
        for kv in range(A_KV_HEADS):
            k2 = jnp.where(masks[kv], kx, kr).astype(BF16)
            v2 = jnp.where(masks[kv], vx, vr)
            for pp in range(pairs_per_kv):
                pi = kv * pairs_per_kv + pp
                sl = slice(pi * LANES, (pi + 1) * LANES)
                out, _ = _packed_attention(q_ref[rows, sl], k2, v2, bias_ref[dead, pi], masks,
                                           (sink_ref[2 * pi], sink_ref[2 * pi + 1]))
                o_ref[rows, sl] = out.astype(o_ref.dtype)


def _band_attn_a(h, bias, sink, *, batch, seq, nblk):
    chunk = BLOCK * nblk
    nc = seq // chunk
    qw, kw = A_OUT_W, A_KV_W

    def cur(b, i):
        return b * nc + i

    def prev(b, i):
        return jnp.maximum((b * nc + i) * nblk - 1, 0)

    return pl.pallas_call(
        _band_attn_kernel,
        grid=(batch, nc),
        in_specs=[pl.BlockSpec((chunk, qw), lambda b, i: (cur(b, i), C_QA // qw)),
                  pl.BlockSpec((chunk, kw), lambda b, i: (cur(b, i), C_KA // kw)),
                  pl.BlockSpec((BLOCK, kw), lambda b, i: (prev(b, i), C_KA // kw)),
                  pl.BlockSpec((chunk, kw), lambda b, i: (cur(b, i), C_VA // kw)),
                  pl.BlockSpec((BLOCK, kw), lambda b, i: (prev(b, i), C_VA // kw)),
                  pl.BlockSpec(bias.shape, lambda b, i: (0, 0, 0, 0)),
                  pl.BlockSpec(memory_space=pltpu.SMEM)],
        out_specs=pl.BlockSpec((chunk, qw), lambda b, i: (cur(b, i), 0)),
        out_shape=jax.ShapeDtypeStruct((batch * seq, qw), BF16),
        compiler_params=_cparams(("parallel", "arbitrary")),
        name="band_attn_a",
    )(h, h, h, h, h, bias, sink)


def _dil_attn_kernel(*refs, dil, nblk, unroll, has_prev, n_merge):
    refs = list(refs)
    q_ref, kc_ref, vc_ref = refs[:3]
    del refs[:3]
    if has_prev:
        kp_ref, vp_ref = refs[:2]
        del refs[:2]
    bias_ref = refs.pop(0)
    others = []
    if n_merge:
        others = [(refs[2 * g], refs[2 * g + 1]) for g in range(n_merge)]
        del refs[:2 * n_merge]
        (o_ref,) = refs
    else:
        o_ref, lse_ref = refs
    first = (pl.program_id(2) == 0).astype(jnp.int32)
    masks = _head_lane_masks()

    def rows(r, j):
        return pl.ds(j * BLOCK * dil + r, BLOCK, stride=dil) if dil > 1 else pl.ds(j * BLOCK, BLOCK)

    def tile(r, j):
        cur = rows(r, j)
        if not has_prev:
            k2, v2, bias = kc_ref[cur, :].astype(BF16), vc_ref[cur, :], bias_ref[0, 0][:, BLOCK:]
        else:
            if j == 0:
                kp, vp, dead = kp_ref[rows(r, 0), :], vp_ref[rows(r, 0), :], first
            else:
                kp, vp, dead = kc_ref[rows(r, j - 1), :], vc_ref[rows(r, j - 1), :], 0
            k2 = jnp.concatenate([kp, kc_ref[cur, :]], axis=0).astype(BF16)
            v2 = jnp.concatenate([vp, vc_ref[cur, :]], axis=0)
            bias = bias_ref[dead, 0]
        out, lse = _packed_attention(q_ref[cur, :], k2, v2, bias, masks, None)
        if not n_merge:
            o_ref[cur, :] = out
            lse_ref[cur, :] = lse
            return
        lses = [lse] + [l_ref[cur, :] for _, l_ref in others]
        outs = [out] + [og_ref[cur, :] for og_ref, _ in others]
        mx = functools.reduce(jnp.maximum, lses)
        es = [jnp.exp(l - mx) for l in lses]
        o_ref[cur, :] = (sum(e * o for e, o in zip(es, outs)) / sum(es)).astype(o_ref.dtype)

    def body(it, carry):
        for u in range(unroll):
            for j in range(nblk):
                tile(it * unroll + u, j)
        return carry

    if dil == unroll:
        body(0, 0)
    else:
        lax.fori_loop(0, dil // unroll, body, 0)


def _dil_attn(h, bias, *, batch, seq, dil, nblk, cq, ck, cv, merge=()):
    band = BLOCK * dil
    chunk = band * nblk
    nc = seq // chunk
    pairs = B_OUT_W // LANES

    def cur(b, p, i):
        return b * nc + i

    def prev(b, p, i):
        return jnp.maximum((b * nc + i) * nblk - 1, 0)

    o_spec = pl.BlockSpec((chunk, LANES), lambda b, p, i: (cur(b, p, i), p))
    o_shape = jax.ShapeDtypeStruct((batch * seq, B_OUT_W), F32)
    has_prev = seq > band
    in_specs = [pl.BlockSpec((chunk, LANES), lambda b, p, i: (cur(b, p, i), cq // LANES + p)),
                pl.BlockSpec((chunk, LANES), lambda b, p, i: (cur(b, p, i), ck // LANES + p)),
                pl.BlockSpec((chunk, LANES), lambda b, p, i: (cur(b, p, i), cv // LANES + p))]
    if has_prev:
        in_specs += [pl.BlockSpec((band, LANES), lambda b, p, i: (prev(b, p, i), ck // LANES + p)),
                     pl.BlockSpec((band, LANES), lambda b, p, i: (prev(b, p, i), cv // LANES + p))]
    n_h = len(in_specs)
    in_specs.append(pl.BlockSpec((2, 1, 2 * BLOCK, 2 * BLOCK), lambda b, p, i: (0, p, 0, 0)))
    others = [a for pair in merge for a in pair]
    in_specs += [o_spec] * len(others)
    if merge:
        assert dil == 1, "the bf16 output needs unstrided stores"
        out_specs, out_shape = o_spec, jax.ShapeDtypeStruct((batch * seq, B_OUT_W), BF16)
    else:
        out_specs, out_shape = (o_spec, o_spec), (o_shape, o_shape)
    return pl.pallas_call(
        functools.partial(_dil_attn_kernel, dil=dil, nblk=nblk, unroll=min(dil, 8 // nblk), has_prev=has_prev,
                          n_merge=len(merge)),
        grid=(batch, pairs, nc),
        in_specs=in_specs,
        out_specs=out_specs,
        out_shape=out_shape,
        compiler_params=_cparams(("parallel", "parallel", "arbitrary")),
        name=f"dil_attn_d{dil}",
    )(*([h] * n_h), bias, *others)


def _kv_tail_kernel(k_ref, v_ref, o_ref):
    cw = k_ref.shape[1]
    o_ref[0, 0:cw, :] = k_ref[...].T
    o_ref[0, cw:2 * cw, :] = v_ref[...].T


def _kv_tail(h, *, batch, seq, win, cw, ck, cv):
    rows = min(win, 8 * BLOCK)
    nblk = win // rows
    base = (seq - win) // rows
    per = seq // rows
    return pl.pallas_call(
        _kv_tail_kernel,
        grid=(batch, nblk),
        in_specs=[pl.BlockSpec((rows, cw), lambda b, i: (b * per + base + i, ck // cw)),
                  pl.BlockSpec((rows, cw), lambda b, i: (b * per + base + i, cv // cw))],
        out_specs=pl.BlockSpec((1, 2 * cw, rows), lambda b, i: (b, 0, i)),
        out_shape=jax.ShapeDtypeStruct((batch, 2 * cw, win), F32),
        compiler_params=_cparams(("parallel", "parallel")),
        name=f"kv_tail_w{win}_c{cw}",
    )(h, h)


def _shift_window(c_ref, cout_ref, knf, vnf, lo):
    lane = lax.broadcasted_iota(jnp.int32, (1, LANES), 1)
    keep = lane < LANES - DEC_T
    to_tail = lax.rem(2 * LANES - DEC_T - lo, LANES)
    new_tail = jnp.concatenate([pltpu.roll(knf, to_tail, 1), pltpu.roll(vnf, to_tail, 1)], axis=0)
    nlb = c_ref.shape[2] // LANES
    nxt = pltpu.roll(c_ref[0, :, 0:LANES], LANES - DEC_T, 1)
    for j in range(nlb):
        cur = nxt
        if j + 1 < nlb:
            nxt = pltpu.roll(c_ref[0, :, (j + 1) * LANES:(j + 2) * LANES], LANES - DEC_T, 1)
        else:
            nxt = new_tail
        cout_ref[0, :, j * LANES:(j + 1) * LANES] = jnp.where(keep, cur, nxt)


def _pad_rows(x):
    return jnp.concatenate([x, jnp.zeros((LANES - SUBLANES, x.shape[1]), F32)], axis=0).astype(BF16)


def _step_attend_a(c_ref, q, knt, vnt, bc_ref, bn_ref, sink_ref, o_scr):
    kvw = A_KV_W
    group = A_Q_HEADS // A_KV_HEADS
    for kv in range(A_KV_HEADS):
        ksl = slice(kv * HEAD_DIM, (kv + 1) * HEAD_DIM)
        kt = c_ref[0, ksl, :].astype(BF16)
        vt = c_ref[0, kvw + kv * HEAD_DIM:kvw + (kv + 1) * HEAD_DIM, :].astype(BF16)
        heads = range(kv * group, (kv + 1) * group)
        qs = jnp.concatenate([q[:, h * HEAD_DIM:(h + 1) * HEAD_DIM] for h in heads], axis=0).astype(BF16)
        s = jnp.dot(qs, kt, preferred_element_type=F32) + bc_ref[kv]
        sn = lax.dot_general(qs, _pad_rows(knt[:, ksl]), (((1,), (1,)), ((), ())), preferred_element_type=F32) + bn_ref[kv]
        sink = sink_ref[kv][:, 0:1]
        m = jnp.maximum(jnp.maximum(jnp.max(s, axis=-1, keepdims=True), jnp.max(sn, axis=-1, keepdims=True)), sink)
        p = jnp.exp(s - m)
        pn = jnp.exp(sn - m)
        l = jnp.sum(p, axis=-1, keepdims=True) + jnp.sum(pn, axis=-1, keepdims=True) + jnp.exp(sink - m)
        o = lax.dot_general(p.astype(BF16), vt, (((1,), (1,)), ((), ())), preferred_element_type=F32)
        o = (o + jnp.dot(pn.astype(BF16), _pad_rows(vnt[:, ksl]), preferred_element_type=F32)) / l
        for g, h in enumerate(heads):
            o_scr[:, h * HEAD_DIM:(h + 1) * HEAD_DIM] = o[g * SUBLANES:(g + 1) * SUBLANES, :]


def _step_attend_b(c_ref, q, knt, vnt, bc_ref, bn_ref, head_lanes):
    kw = B_OUT_W
    qx = jnp.concatenate([jnp.where(mk, q, 0.0) for mk in head_lanes], axis=0).astype(BF16)
    kt = c_ref[0, 0:kw, :].astype(BF16)
    vt = c_ref[0, kw:2 * kw, :].astype(BF16)
    s = jnp.dot(qx, kt, preferred_element_type=F32) + bc_ref[...]
    sn = lax.dot_general(qx, _pad_rows(knt), (((1,), (1,)), ((), ())), preferred_element_type=F32) + bn_ref[...]
    m = jnp.maximum(jnp.max(s, axis=-1, keepdims=True), jnp.max(sn, axis=-1, keepdims=True))
    p = jnp.exp(s - m)
    pn = jnp.exp(sn - m)
    l = jnp.sum(p, axis=-1, keepdims=True) + jnp.sum(pn, axis=-1, keepdims=True)
    ox = lax.dot_general(p.astype(BF16), vt, (((1,), (1,)), ((), ())), preferred_element_type=F32)
    ox = ox + jnp.dot(pn.astype(BF16), _pad_rows(vnt), preferred_element_type=F32)
    o = jnp.zeros((SUBLANES, kw), F32)
    lrow = jnp.ones((SUBLANES, kw), F32)
    mrow = jnp.zeros((SUBLANES, kw), F32)
    for h, mk in enumerate(head_lanes):
        rows = slice(h * SUBLANES, (h + 1) * SUBLANES)
        o = jnp.where(mk, ox[rows], o)
        lrow = jnp.where(mk, l[rows], lrow)
        mrow = jnp.where(mk, m[rows], mrow)
    return o / lrow, mrow + jnp.log(lrow)


def _step_kernel(ca_ref, c1_ref, c2_ref, c3_ref, hq_ref, fa_ref, fb_ref,
                 bca_ref, bna_ref, sink_ref, bc1_ref, bn1_ref, bc2_ref, bn2_ref, bc3_ref, bn3_ref,
                 na_ref, n1_ref, n2_ref, n3_ref, oa_ref, ob_ref, oa_scr):
    b = pl.program_id(0)
    lo = DEC_T * lax.rem(b, LANES // DEC_T)

    _shift_window(ca_ref, na_ref, fa_ref[0:A_KV_W, :], fa_ref[A_KV_W:2 * A_KV_W, :], lo)
    groups = ((c1_ref, n1_ref, bc1_ref, bn1_ref), (c2_ref, n2_ref, bc2_ref, bn2_ref), (c3_ref, n3_ref, bc3_ref, bn3_ref))
    n_groups = len(groups)
    for g, (c_ref, n_ref, _, _) in enumerate(groups):
        _shift_window(c_ref, n_ref, fb_ref[g * B_OUT_W:(g + 1) * B_OUT_W, :],
                      fb_ref[(n_groups + g) * B_OUT_W:(n_groups + g + 1) * B_OUT_W, :], lo)

    _step_attend_a(ca_ref, hq_ref[:, C_QA:C_QA + A_OUT_W], hq_ref[:, C_KA:C_KA + A_KV_W], hq_ref[:, C_VA:C_VA + A_KV_W],
                   bca_ref, bna_ref, sink_ref, oa_scr)

    lane = lax.broadcasted_iota(jnp.int32, (1, B_OUT_W), 1)
    head_lanes = [lane // HEAD_DIM == h for h in range(B_HEADS)]
    outs, lses = [], []
    for g, (c_ref, _, bc_ref, bn_ref) in enumerate(groups):
        sl = lambda c0: slice(c0 + g * B_OUT_W, c0 + (g + 1) * B_OUT_W)
        o, lse = _step_attend_b(c_ref, hq_ref[:, sl(C_QB)], hq_ref[:, sl(C_KB)], hq_ref[:, sl(C_VB)], bc_ref, bn_ref,
                                head_lanes)
        outs.append(o)
        lses.append(lse)
    mx = functools.reduce(jnp.maximum, lses)
    es = [jnp.exp(lse - mx) for lse in lses]
    ob = sum(e * o for e, o in zip(es, outs)) / sum(es)

    half = lax.rem(b, 2)
    mine = lax.broadcasted_iota(jnp.int32, (SUBLANES, 1), 0) // DEC_T == half

    @pl.when(half == 0)
    def _():
        oa_ref[...] = jnp.where(mine, oa_scr[...], 0.0)
        ob_ref[...] = jnp.where(mine, ob, 0.0)

    @pl.when(half == 1)
    def _():
        oa_ref[...] = jnp.where(mine, oa_scr[...], oa_ref[...])
        ob_ref[...] = jnp.where(mine, ob, ob_ref[...])


def _step_attn(caches, hs, ht, tables):
    db = caches[0].shape[0]
    per_tile = LANES // DEC_T
    per_blk = SUBLANES // DEC_T
    n_kb = len(B_PATTERNS) * B_OUT_W
    cache_specs = [pl.BlockSpec((1,) + c.shape[1:], lambda b: (b, 0, 0)) for c in caches]
    in_specs = cache_specs + [
        pl.BlockSpec((SUBLANES, QKV_W), lambda b: (b // per_blk, 0)),
        pl.BlockSpec((2 * A_KV_W, LANES), lambda b: (C_KA // (2 * A_KV_W), b // per_tile)),
        pl.BlockSpec((2 * n_kb, LANES), lambda b: (C_KB // (2 * n_kb), b // per_tile)),
    ] + [pl.BlockSpec(t.shape, lambda b, nd=t.ndim: (0,) * nd) for t in tables]
    o_specs = [pl.BlockSpec((SUBLANES, A_OUT_W), lambda b: (b // per_blk, 0)),
               pl.BlockSpec((SUBLANES, B_OUT_W), lambda b: (b // per_blk, 0))]
    return pl.pallas_call(
        _step_kernel,
        grid=(db,),
        in_specs=in_specs,
        out_specs=tuple(cache_specs + o_specs),
        out_shape=tuple([jax.ShapeDtypeStruct(c.shape, F32) for c in caches]
                        + [jax.ShapeDtypeStruct((db * DEC_T, A_OUT_W), F32), jax.ShapeDtypeStruct((db * DEC_T, B_OUT_W), F32)]),
        scratch_shapes=[pltpu.VMEM((SUBLANES, A_OUT_W), F32)],
        compiler_params=_cparams(("arbitrary",)),
        name="step_attn",
    )(*caches, hs, ht, ht, *tables)


PREP_W = C_QB - C_PAD


def _prep_w_in_kernel(w_ref, o_ref):
    j = pl.program_id(0)
    is_q = (j < C_KA // PREP_W) | ((j >= C_QB // PREP_W) & (j < C_KB // PREP_W))
    scale = jnp.where(is_q, SCALE, 1.0)
    o_ref[...] = jnp.where(j == C_PAD // PREP_W, 0.0, w_ref[...] * scale).astype(o_ref.dtype)


def _prep_w_in(w):
    pad_blk = C_PAD // PREP_W
    return pl.pallas_call(
        _prep_w_in_kernel,
        grid=(HW // PREP_W,),
        in_specs=[pl.BlockSpec((w.shape[0], PREP_W), lambda j: (0, jnp.where(j > pad_blk, j - 1, jnp.minimum(j, pad_blk - 1))))],
        out_specs=pl.BlockSpec((w.shape[0], PREP_W), lambda j: (0, j)),
        out_shape=jax.ShapeDtypeStruct((w.shape[0], HW), BF16),
        compiler_params=_cparams(("parallel",)),
        name="prep_w_in",
    )(w)


def _to_feature_major(cache):
    db, win = cache.shape[:2]
    return jnp.transpose(cache, (0, 2, 3, 4, 1)).reshape(db, -1, win)


def _from_feature_major(ct, heads):
    n, _, win = ct.shape
    return jnp.transpose(ct.reshape(n, 2, heads, HEAD_DIM, win), (0, 4, 1, 2, 3))[None]


def _tail_layers(x, h, oa, ob, w_oa, w_ob, w_out, ln1_g, ln1_b, w_ffn_in, w_ffn_out, ln2_g, ln2_b):
    t = x.shape[0]
    tm = min(1024, t)
    mixin = _gate_proj(oa, ob, h, w_oa, w_ob, tm, 1024)
    h1, h1b = _mm_res_ln(mixin, w_out, x, ln1_g, ln1_b, min(512, t), D_MODEL, "out_ln1", also_bf16=True)
    u = _ffn_in(h1b, w_ffn_in, min(2048, t), 512)
    return _mm_res_ln(u, w_ffn_out, h1, ln2_g, ln2_b, tm, D_FF // 4, "ffn_out_ln2")


def kernel(x_prompt, x_sample, cache_a, cache_b1, cache_b2, cache_b3, rel_bias, w_in, a_sink, w_oa, w_ob,
           w_out, ln1_g, ln1_b, w_ffn_in, w_ffn_out, ln2_g, ln2_b):
    batch, seq, _ = x_prompt.shape
    db, dt, _ = x_sample.shape
    assert dt == DEC_T and w_in.shape[0] == DEPTH and db % (SUBLANES // DEC_T) == 0
    tp, ts = batch * seq, db * dt

    w_in_b = _prep_w_in(w_in[0])
    weights = (w_oa[0].astype(BF16), w_ob[0].astype(BF16), w_out[0].astype(BF16), ln1_g, ln1_b,
               w_ffn_in[0], w_ffn_out[0].astype(BF16), ln2_g, ln2_b)
    sink = a_sink[0].astype(F32)
    b_h0 = [A_Q_HEADS + g * B_HEADS for g in range(len(B_PATTERNS))]
    b_cols = [(C_QB + g * B_OUT_W, C_KB + g * B_OUT_W, C_VB + g * B_OUT_W) for g in range(len(B_PATTERNS))]

    xp = x_prompt.reshape(tp, D_MODEL)
    hp = _in_proj(xp, w_in_b, 1024, 2048)
    bias_a = _packed_rows(_band_bias(rel_bias, A_WINDOW - 1, 1, 0, A_Q_HEADS), A_Q_HEADS // 2, 1)
    oa = _band_attn_a(hp, bias_a, sink, batch=batch, seq=seq, nblk=2)
    others, ob = [], None
    for g, (win, dil) in sorted(enumerate(B_PATTERNS), key=lambda e: -e[1][1]):
        cq, ck, cv = b_cols[g]
        bias_g = _packed_rows(_band_bias(rel_bias, win // dil, dil, b_h0[g], B_HEADS), B_HEADS // 2, 1)
        res = _dil_attn(hp, bias_g, batch=batch, seq=seq, dil=dil, nblk={1: 8, 4: 2, 16: 1}[dil], cq=cq, ck=ck, cv=cv,
                        merge=others if dil == 1 else ())
        if dil == 1:
            ob = res
        else:
            others.append(res)
    yp = _tail_layers(xp, hp, oa, ob, *weights).reshape(batch, seq, D_MODEL)

    new_a_p = _from_feature_major(_kv_tail(hp, batch=batch, seq=seq, win=min(A_WINDOW, seq), cw=A_KV_W, ck=C_KA, cv=C_VA),
                                  A_KV_HEADS)
    new_b_p = [_from_feature_major(_kv_tail(hp, batch=batch, seq=seq, win=min(win, seq), cw=B_OUT_W,
                                            ck=b_cols[g][1], cv=b_cols[g][2]), B_HEADS)
               for g, (win, dil) in enumerate(B_PATTERNS)]

    xs = x_sample.reshape(ts, D_MODEL)
    hs = _in_proj(xs, w_in_b, ts, 1024)
    lanes = -(-ts // LANES) * LANES
    ht = jnp.pad(hs[:, :QKV_W].T, ((0, 0), (0, lanes - ts)))

    group_a = A_Q_HEADS // A_KV_HEADS
    sink_rows = jnp.broadcast_to(jnp.repeat(sink.reshape(A_KV_HEADS, group_a), SUBLANES, axis=1)[:, :, None],
                                 (A_KV_HEADS, group_a * SUBLANES, LANES))
    tables = list(_step_bias(rel_bias, cache_a.shape[2], 1, 0, A_Q_HEADS, A_KV_HEADS, True)) + [sink_rows]
    for g, ((win, dil), cache) in enumerate(zip(B_PATTERNS, (cache_b1, cache_b2, cache_b3))):
        bc, bn = _step_bias(rel_bias, cache.shape[2], dil, b_h0[g], B_HEADS, 1, False)
        tables += [bc[0], bn[0]]
    caches = [_to_feature_major(c[0]) for c in (cache_a, cache_b1, cache_b2, cache_b3)]
    new_a_t, new_b1_t, new_b2_t, new_b3_t, oa_s, ob_s = _step_attn(caches, hs, ht, tables)
    ys = _tail_layers(xs, hs, oa_s.astype(BF16), ob_s.astype(BF16), *weights).reshape(db, dt, D_MODEL)

    return (yp, ys, new_a_p, new_b_p[0], new_b_p[1], new_b_p[2],
            _from_feature_major(new_a_t, A_KV_HEADS), _from_feature_major(new_b1_t, B_HEADS),
            _from_feature_major(new_b2_t, B_HEADS), _from_feature_major(new_b3_t, B_HEADS))
```

```python
import functools
import math

import numpy as np
import jax
import jax.numpy as jnp
from jax import lax
from jax.experimental import pallas as pl
from jax.experimental.pallas import tpu as pltpu

F32 = jnp.float32
BF16 = jnp.bfloat16

D_MODEL = 2048
HEAD_DIM = 64
A_WINDOW = 128
A_Q_HEADS = 16
A_KV_HEADS = 2
B_PATTERNS = ((128, 1), (512, 4), (2048, 16))
B_HEADS = 8
NUM_BUCKETS = 32
REL_MAX_DIST = 2048
BLOCK = 128
D_FF = 5632
DEPTH = 1
ALPHA = (2 * DEPTH) ** 0.25
SCALE = HEAD_DIM ** -0.5
LN_EPS = 1e-5
NEG = -1e30
LANES = 128
SUBLANES = 8
DEC_T = 4

A_OUT_W = A_Q_HEADS * HEAD_DIM
A_KV_W = A_KV_HEADS * HEAD_DIM
B_OUT_W = B_HEADS * HEAD_DIM
C_QA, C_KA, C_VA, C_PAD, C_QB, C_KB, C_VB, C_GA, C_GB = 0, 1024, 1152, 1280, 1536, 3072, 4608, 6144, 8192
HW = 10240
QKV_W = C_GA
VMEM_LIMIT = 56 * 1024 * 1024


def _cparams(sem):
    return pltpu.CompilerParams(dimension_semantics=sem, vmem_limit_bytes=VMEM_LIMIT)


def _bucket_np(dist):
    d = np.maximum(np.asarray(dist, np.int64), 0)
    max_exact = NUM_BUCKETS // 2
    ratio = np.maximum(d, max_exact).astype(np.float32) / np.float32(max_exact)
    large = max_exact + (np.log(ratio) / np.float32(math.log(REL_MAX_DIST / max_exact))
                         * np.float32(NUM_BUCKETS - max_exact)).astype(np.int32)
    return np.where(d < max_exact, d, np.minimum(large, NUM_BUCKETS - 1)).astype(np.int32)


def _bias_by_dist(rel_bias, dists, h0, nh):
    return jnp.take(rel_bias[:, h0:h0 + nh], jnp.asarray(_bucket_np(dists)), axis=0).T.astype(F32)


def _band_bias(rel_bias, max_dist, dil, h0, nh):
    u = BLOCK - np.arange(2 * BLOCK)
    w = jnp.where(jnp.asarray((u >= 0) & (u <= max_dist))[None],
                  _bias_by_dist(rel_bias, np.maximum(u, 0) * dil, h0, nh), NEG)
    x = jnp.concatenate([w, jnp.full((nh, 1), NEG, F32)], axis=1)
    band = jnp.tile(x, (1, BLOCK))[:, :BLOCK * 2 * BLOCK].reshape(nh, BLOCK, 2 * BLOCK)
    return jnp.stack([band, jnp.where(jnp.asarray(np.arange(2 * BLOCK) < BLOCK), NEG, band)])


def _packed_rows(per_head, groups, pairs):
    lead, tail = per_head.shape[:-3], per_head.shape[-1]
    t = per_head.reshape(*lead, groups, pairs, 2, BLOCK, tail)
    t = jnp.swapaxes(t, -4, -3)
    return t.reshape(*lead, groups, 2 * pairs * BLOCK, tail)


def _step_bias(rel_bias, win, dil, h0, nh, n_kv, is_a):
    neg = lambda n: jnp.full((nh, n), NEG, F32)
    rows_c = []
    if is_a:
        rev = _bias_by_dist(rel_bias, np.arange(A_WINDOW - 1, -1, -1), h0, nh)
        for t in range(DEC_T):
            rows_c.append(jnp.concatenate([neg(t + 1), rev[:, :win - t - 1]], axis=1))
    elif dil == 1:
        rev = _bias_by_dist(rel_bias, np.arange(win, 0, -1), h0, nh)
        for t in range(DEC_T):
            rows_c.append(jnp.concatenate([neg(t), rev[:, :win - t]], axis=1))
    else:
        rev = _bias_by_dist(rel_bias, np.arange(win // dil, 0, -1) * dil, h0, nh)
        for t in range(DEC_T):
            slots = [rev[:, :, None] if r == t else jnp.full((nh, win // dil, 1), NEG, F32) for r in range(dil)]
            rows_c.append(jnp.concatenate(slots, axis=2).reshape(nh, win))
    bc = jnp.stack(rows_c * 2, axis=1)

    i = np.arange(SUBLANES)[:, None]
    j = np.arange(LANES)[None, :]
    dn = i % DEC_T - j % DEC_T
    vn = (j < SUBLANES) & (i // DEC_T == j // DEC_T) & (dn >= 0)
    if not is_a:
        vn &= dn % dil == 0
    near = _bias_by_dist(rel_bias, np.arange(DEC_T), h0, nh)
    bn = jnp.where(jnp.asarray(vn)[None], near[:, np.clip(dn, 0, DEC_T - 1)], NEG)
    g8 = (nh // n_kv) * SUBLANES
    return bc.reshape(n_kv, g8, win), bn.reshape(n_kv, g8, LANES)


def _inproj_kernel(x_ref, w_ref, o_ref, xb_ref):
    @pl.when(pl.program_id(1) == 0)
    def _():
        xb_ref[...] = x_ref[...].astype(BF16)

    o_ref[...] = jnp.dot(xb_ref[...], w_ref[...], preferred_element_type=F32)


def _in_proj(x, w, tm, tn):
    t, k = x.shape
    n = w.shape[1]
    return pl.pallas_call(
        _inproj_kernel,
        grid=(t // tm, n // tn),
        in_specs=[pl.BlockSpec((tm, k), lambda i, j: (i, 0)),
                  pl.BlockSpec((k, tn), lambda i, j: (0, j))],
        out_specs=pl.BlockSpec((tm, tn), lambda i, j: (i, j)),
        out_shape=jax.ShapeDtypeStruct((t, n), F32),
        scratch_shapes=[pltpu.VMEM((tm, k), BF16)],
        compiler_params=_cparams(("parallel", "arbitrary")),
        name="in_proj",
    )(x, w)


def _gate_proj_kernel(oa_ref, ob_ref, ga_ref, gb_ref, woa_ref, wob_ref, o_ref):
    pa = jnp.dot(oa_ref[...], woa_ref[...], preferred_element_type=F32)
    pb = jnp.dot(ob_ref[...], wob_ref[...], preferred_element_type=F32)
    sa = 0.5 * jnp.tanh(0.5 * ga_ref[...]) + 0.5
    sb = 0.5 * jnp.tanh(0.5 * gb_ref[...]) + 0.5
    o_ref[...] = (sa * pa + sb * pb).astype(o_ref.dtype)


def _gate_proj(oa, ob, h, w_oa, w_ob, tm, tn):
    t = oa.shape[0]
    nj = D_MODEL // tn
    return pl.pallas_call(
        _gate_proj_kernel,
        grid=(t // tm, nj),
        in_specs=[pl.BlockSpec((tm, A_OUT_W), lambda i, j: (i, 0)),
                  pl.BlockSpec((tm, B_OUT_W), lambda i, j: (i, 0)),
                  pl.BlockSpec((tm, tn), lambda i, j: (i, C_GA // tn + j)),
                  pl.BlockSpec((tm, tn), lambda i, j: (i, C_GB // tn + j)),
                  pl.BlockSpec((A_OUT_W, tn), lambda i, j: (0, j)),
                  pl.BlockSpec((B_OUT_W, tn), lambda i, j: (0, j))],
        out_specs=pl.BlockSpec((tm, tn), lambda i, j: (i, j)),
        out_shape=jax.ShapeDtypeStruct((t, D_MODEL), BF16),
        compiler_params=_cparams(("parallel", "arbitrary")),
        name="gate_proj",
    )(oa, ob, h, h, w_oa, w_ob)


def _layer_norm(z, g, b):
    mu = jnp.mean(z, axis=-1, keepdims=True)
    zc = z - mu
    var = jnp.mean(zc * zc, axis=-1, keepdims=True)
    return zc * lax.rsqrt(var + LN_EPS) * g + b


def _mm_res_ln_kernel(a_ref, w_ref, r_ref, g_ref, b_ref, o_ref, ob_ref=None, *, n_chunk):
    k = pl.program_id(1)
    nk = pl.num_programs(1)
    if n_chunk is None:
        sub = min(128, o_ref.shape[0])
        for r0 in range(0, o_ref.shape[0], sub):
            rows = slice(r0, r0 + sub)
            mix = jnp.dot(a_ref[rows, :], w_ref[...], preferred_element_type=F32)
            y = _layer_norm(ALPHA * r_ref[rows, :] + mix, g_ref[...], b_ref[...])
            o_ref[rows, :] = y
            if ob_ref is not None:
                ob_ref[rows, :] = y.astype(ob_ref.dtype)
        return

    def accumulate(first):
        a = a_ref[...]
        for c in range(D_MODEL // n_chunk):
            cols = slice(c * n_chunk, (c + 1) * n_chunk)
            part = jnp.dot(a, w_ref[:, cols], preferred_element_type=F32)
            o_ref[:, cols] = part if first else o_ref[:, cols] + part

    pl.when(k == 0)(lambda: accumulate(True))
    pl.when(k > 0)(lambda: accumulate(False))

    @pl.when(k == nk - 1)
    def _():
        rows_per_pass = 256
        for r0 in range(0, o_ref.shape[0], rows_per_pass):
            rows = slice(r0, r0 + rows_per_pass)
            o_ref[rows, :] = _layer_norm(ALPHA * r_ref[rows, :] + o_ref[rows, :], g_ref[...], b_ref[...])


def _mm_res_ln(a, w, res, g, b, tm, tk, name, also_bf16=False):
    t, kdim = a.shape
    assert not also_bf16 or tk == kdim
    o_spec = pl.BlockSpec((tm, D_MODEL), lambda i, k: (i, 0))
    o_shape = jax.ShapeDtypeStruct((t, D_MODEL), F32)
    return pl.pallas_call(
        functools.partial(_mm_res_ln_kernel, n_chunk=None if tk == kdim else 512),
        grid=(t // tm, kdim // tk),
        in_specs=[pl.BlockSpec((tm, tk), lambda i, k: (i, k)),
                  pl.BlockSpec((tk, D_MODEL), lambda i, k: (k, 0)),
                  pl.BlockSpec((tm, D_MODEL), lambda i, k: (i, 0)),
                  pl.BlockSpec((1, D_MODEL), lambda i, k: (0, 0)),
                  pl.BlockSpec((1, D_MODEL), lambda i, k: (0, 0))],
        out_specs=(o_spec, o_spec) if also_bf16 else o_spec,
        out_shape=(o_shape, jax.ShapeDtypeStruct((t, D_MODEL), BF16)) if also_bf16 else o_shape,
        compiler_params=_cparams(("parallel", "arbitrary")),
        name=name,
    )(a, w, res, g, b)


def _ffn_in_kernel(h_ref, wg_ref, wu_ref, o_ref, *, sub):
    wg = wg_ref[...].astype(BF16)
    wu = wu_ref[...].astype(BF16)
    for r0 in range(0, o_ref.shape[0], sub):
        rows = slice(r0, r0 + sub)
        hb = h_ref[rows, :]
        gate = jnp.dot(hb, wg, preferred_element_type=F32)
        up = jnp.dot(hb, wu, preferred_element_type=F32)
        o_ref[rows, :] = (gate * jax.nn.sigmoid(gate) * up).astype(o_ref.dtype)


def _ffn_in(h1b, w_ffn_in, tm, tn):
    t = h1b.shape[0]
    nj = D_FF // tn
    return pl.pallas_call(
        functools.partial(_ffn_in_kernel, sub=min(1024, tm)),
        grid=(t // tm, nj),
        in_specs=[pl.BlockSpec((tm, D_MODEL), lambda i, j: (i, 0)),
                  pl.BlockSpec((D_MODEL, tn), lambda i, j: (0, j)),
                  pl.BlockSpec((D_MODEL, tn), lambda i, j: (0, nj + j))],
        out_specs=pl.BlockSpec((tm, tn), lambda i, j: (i, j)),
        out_shape=jax.ShapeDtypeStruct((t, D_FF), BF16),
        compiler_params=_cparams(("parallel", "arbitrary")),
        name="ffn_in",
    )(h1b, w_ffn_in, w_ffn_in)


def _head_lane_masks():
    lane = lax.broadcasted_iota(jnp.int32, (1, LANES), 1)
    return [lane < HEAD_DIM, lane >= HEAD_DIM]


def _packed_attention(q, k2b, v2, bias, masks, sinks):
    q2 = jnp.concatenate([jnp.where(mk, q, 0.0) for mk in masks], axis=0).astype(BF16)
    s = lax.dot_general(q2, k2b, (((1,), (1,)), ((), ())), preferred_element_type=F32) + bias
    m = jnp.max(s, axis=-1, keepdims=True)
    if sinks is not None:
        sink = jnp.where(lax.broadcasted_iota(jnp.int32, (2 * BLOCK, 1), 0) < BLOCK, sinks[0], sinks[1])
        m = jnp.maximum(m, sink)
    p = jnp.exp(s - m).astype(BF16)
    nd = jnp.zeros((BLOCK, 2 * LANES), F32)
    for hh, mk in enumerate(masks):
        ones = jnp.broadcast_to(jnp.where(mk, 1.0, 0.0), v2.shape)
        w = jnp.concatenate([jnp.where(mk, v2, 0.0), ones], axis=1).astype(BF16)
        nd = nd + jnp.dot(p[hh * BLOCK:(hh + 1) * BLOCK], w, preferred_element_type=F32)
    den = nd[:, LANES:]
    if sinks is not None:
        es = jnp.exp(sink - m)
        den = den + jnp.where(masks[0], es[:BLOCK], es[BLOCK:])
    return nd[:, :LANES] / den, jnp.where(masks[0], m[:BLOCK], m[BLOCK:]) + jnp.log(den)


def _band_attn_kernel(q_ref, kc_ref, kp_ref, vc_ref, vp_ref, bias_ref, sink_ref, o_ref):
    first = (pl.program_id(1) == 0).astype(jnp.int32)
    masks = _head_lane_masks()
    pairs_per_kv = (A_Q_HEADS // A_KV_HEADS) // 2
    for j in range(q_ref.shape[0] // BLOCK):
        rows = slice(j * BLOCK, (j + 1) * BLOCK)
        if j == 0:
            kp, vp, dead = kp_ref[...], vp_ref[...], first
        else:
            kp, vp, dead = kc_ref[(j - 1) * BLOCK:j * BLOCK, :], vc_ref[(j - 1) * BLOCK:j * BLOCK, :], 0
        kx = jnp.concatenate([kp, kc_ref[rows, :]], axis=0)
        vx = jnp.concatenate([vp, vc_ref[rows, :]], axis=0)
        kr = pltpu.roll(kx, HEAD_DIM, 1)
        vr = pltpu.roll(vx, HEAD_DIM, 1)
        for kv in range(A_KV_HEADS):
            k2 = jnp.where(masks[kv], kx, kr).astype(BF16)
            v2 = jnp.where(masks[kv], vx, vr)
            for pp in range(pairs_per_kv):
                pi = kv * pairs_per_kv + pp
                sl = slice(pi * LANES, (pi + 1) * LANES)
                out, _ = _packed_attention(q_ref[rows, sl], k2, v2, bias_ref[dead, pi], masks,
                                           (sink_ref[2 * pi], sink_ref[2 * pi + 1]))
                o_ref[rows, sl] = out.astype(o_ref.dtype)


def _band_attn_a(h, bias, sink, *, batch, seq, nblk):
    chunk = BLOCK * nblk
    nc = seq // chunk
    qw, kw = A_OUT_W, A_KV_W

    def cur(b, i):
        return b * nc + i

    def prev(b, i):
        return jnp.maximum((b * nc + i) * nblk - 1, 0)

    return pl.pallas_call(
        _band_attn_kernel,
        grid=(batch, nc),
        in_specs=[pl.BlockSpec((chunk, qw), lambda b, i: (cur(b, i), C_QA // qw)),
                  pl.BlockSpec((chunk, kw), lambda b, i: (cur(b, i), C_KA // kw)),
                  pl.BlockSpec((BLOCK, kw), lambda b, i: (prev(b, i), C_KA // kw)),
                  pl.BlockSpec((chunk, kw), lambda b, i: (cur(b, i), C_VA // kw)),
                  pl.BlockSpec((BLOCK, kw), lambda b, i: (prev(b, i), C_VA // kw)),
                  pl.BlockSpec(bias.shape, lambda b, i: (0, 0, 0, 0)),
                  pl.BlockSpec(memory_space=pltpu.SMEM)],
        out_specs=pl.BlockSpec((chunk, qw), lambda b, i: (cur(b, i), 0)),
        out_shape=jax.ShapeDtypeStruct((batch * seq, qw), BF16),
        compiler_params=_cparams(("parallel", "arbitrary")),
        name="band_attn_a",
    )(h, h, h, h, h, bias, sink)


def _dil_attn_kernel(*refs, dil, nblk, unroll, has_prev, n_merge):
    refs = list(refs)
    q_ref, kc_ref, vc_ref = refs[:3]
    del refs[:3]
    if has_prev:
        kp_ref, vp_ref = refs[:2]
        del refs[:2]
    bias_ref = refs.pop(0)
    others = []
    if n_merge:
        others = [(refs[2 * g], refs[2 * g + 1]) for g in range(n_merge)]
        del refs[:2 * n_merge]
        (o_ref,) = refs
    else:
        o_ref, lse_ref = refs
    first = (pl.program_id(2) == 0).astype(jnp.int32)
    masks = _head_lane_masks()

    def rows(r, j):
        return pl.ds(j * BLOCK * dil + r, BLOCK, stride=dil) if dil > 1 else pl.ds(j * BLOCK, BLOCK)

    def tile(r, j):
        cur = rows(r, j)
        if not has_prev:
            k2, v2, bias = kc_ref[cur, :].astype(BF16), vc_ref[cur, :], bias_ref[0, 0][:, BLOCK:]
        else:
            if j == 0:
                kp, vp, dead = kp_ref[rows(r, 0), :], vp_ref[rows(r, 0), :], first
            else:
                kp, vp, dead = kc_ref[rows(r, j - 1), :], vc_ref[rows(r, j - 1), :], 0
            k2 = jnp.concatenate([kp, kc_ref[cur, :]], axis=0).astype(BF16)
            v2 = jnp.concatenate([vp, vc_ref[cur, :]], axis=0)
            bias = bias_ref[dead, 0]
        out, lse = _packed_attention(q_ref[cur, :], k2, v2, bias, masks, None)
        if not n_merge:
            o_ref[cur, :] = out
            lse_ref[cur, :] = lse
            return
        lses = [lse] + [l_ref[cur, :] for _, l_ref in others]
        outs = [out] + [og_ref[cur, :] for og_ref, _ in others]
        mx = functools.reduce(jnp.maximum, lses)
        es = [jnp.exp(l - mx) for l in lses]
        o_ref[cur, :] = (sum(e * o for e, o in zip(es, outs)) / sum(es)).astype(o_ref.dtype)

    def body(it, carry):
        for u in range(unroll):
            for j in range(nblk):
                tile(it * unroll + u, j)
        return carry

    if dil == unroll:
        body(0, 0)
    else:
        lax.fori_loop(0, dil // unroll, body, 0)


def _dil_attn(h, bias, *, batch, seq, dil, nblk, cq, ck, cv, merge=()):
    band = BLOCK * dil
    chunk = band * nblk
    nc = seq // chunk
    pairs = B_OUT_W // LANES

    def cur(b, p, i):
        return b * nc + i

    def prev(b, p, i):
        return jnp.maximum((b * nc + i) * nblk - 1, 0)

    o_spec = pl.BlockSpec((chunk, LANES), lambda b, p, i: (cur(b, p, i), p))
    o_shape = jax.ShapeDtypeStruct((batch * seq, B_OUT_W), F32)
    has_prev = seq > band
    in_specs = [pl.BlockSpec((chunk, LANES), lambda b, p, i: (cur(b, p, i), cq // LANES + p)),
                pl.BlockSpec((chunk, LANES), lambda b, p, i: (cur(b, p, i), ck // LANES + p)),
                pl.BlockSpec((chunk, LANES), lambda b, p, i: (cur(b, p, i), cv // LANES + p))]
    if has_prev:
        in_specs += [pl.BlockSpec((band, LANES), lambda b, p, i: (prev(b, p, i), ck // LANES + p)),
                     pl.BlockSpec((band, LANES), lambda b, p, i: (prev(b, p, i), cv // LANES + p))]
    n_h = len(in_specs)
    in_specs.append(pl.BlockSpec((2, 1, 2 * BLOCK, 2 * BLOCK), lambda b, p, i: (0, p, 0, 0)))
    others = [a for pair in merge for a in pair]
    in_specs += [o_spec] * len(others)
    if merge:
        assert dil == 1, "the bf16 output needs unstrided stores"
        out_specs, out_shape = o_spec, jax.ShapeDtypeStruct((batch * seq, B_OUT_W), BF16)
    else:
        out_specs, out_shape = (o_spec, o_spec), (o_shape, o_shape)
    return pl.pallas_call(
        functools.partial(_dil_attn_kernel, dil=dil, nblk=nblk, unroll=max(1, min(dil, 8 // nblk)), has_prev=has_prev,
                          n_merge=len(merge)),
        grid=(batch, pairs, nc),
        in_specs=in_specs,
        out_specs=out_specs,
        out_shape=out_shape,
        compiler_params=_cparams(("parallel", "parallel", "arbitrary")),
        name=f"dil_attn_d{dil}",
    )(*([h] * n_h), bias, *others)


def _kv_tail_kernel(k_ref, v_ref, o_ref):
    cw = k_ref.shape[1]
    o_ref[0, 0:cw, :] = k_ref[...].T
    o_ref[0, cw:2 * cw, :] = v_ref[...].T


def _kv_tail(h, *, batch, seq, win, cw, ck, cv):
    rows = min(win, 8 * BLOCK)
    nblk = win // rows
    base = (seq - win) // rows
    per = seq // rows
    return pl.pallas_call(
        _kv_tail_kernel,
        grid=(batch, nblk),
        in_specs=[pl.BlockSpec((rows, cw), lambda b, i: (b * per + base + i, ck // cw)),
                  pl.BlockSpec((rows, cw), lambda b, i: (b * per + base + i, cv // cw))],
        out_specs=pl.BlockSpec((1, 2 * cw, rows), lambda b, i: (b, 0, i)),
        out_shape=jax.ShapeDtypeStruct((batch, 2 * cw, win), F32),
        compiler_params=_cparams(("parallel", "parallel")),
        name=f"kv_tail_w{win}_c{cw}",
    )(h, h)


def _shift_window(c_ref, cout_ref, knf, vnf, lo):
    lane = lax.broadcasted_iota(jnp.int32, (1, LANES), 1)
    keep = lane < LANES - DEC_T
    to_tail = lax.rem(2 * LANES - DEC_T - lo, LANES)
    new_tail = jnp.concatenate([pltpu.roll(knf, to_tail, 1), pltpu.roll(vnf, to_tail, 1)], axis=0)
    nlb = c_ref.shape[2] // LANES
    nxt = pltpu.roll(c_ref[0, :, 0:LANES], LANES - DEC_T, 1)
    for j in range(nlb):
        cur = nxt
        if j + 1 < nlb:
            nxt = pltpu.roll(c_ref[0, :, (j + 1) * LANES:(j + 2) * LANES], LANES - DEC_T, 1)
        else:
            nxt = new_tail
        cout_ref[0, :, j * LANES:(j + 1) * LANES] = jnp.where(keep, cur, nxt)


def _pad_rows(x):
    return jnp.concatenate([x, jnp.zeros((LANES - SUBLANES, x.shape[1]), F32)], axis=0).astype(BF16)


def _step_attend_a(c_ref, q, knt, vnt, bc_ref, bn_ref, sink_ref, o_scr):
    kvw = A_KV_W
    group = A_Q_HEADS // A_KV_HEADS
    for kv in range(A_KV_HEADS):
        ksl = slice(kv * HEAD_DIM, (kv + 1) * HEAD_DIM)
        kt = c_ref[0, ksl, :].astype(BF16)
        vt = c_ref[0, kvw + kv * HEAD_DIM:kvw + (kv + 1) * HEAD_DIM, :].astype(BF16)
        heads = range(kv * group, (kv + 1) * group)
        qs = jnp.concatenate([q[:, h * HEAD_DIM:(h + 1) * HEAD_DIM] for h in heads], axis=0).astype(BF16)
        s = jnp.dot(qs, kt, preferred_element_type=F32) + bc_ref[kv]
        sn = lax.dot_general(qs, _pad_rows(knt[:, ksl]), (((1,), (1,)), ((), ())), preferred_element_type=F32) + bn_ref[kv]
        sink = sink_ref[kv][:, 0:1]
        m = jnp.maximum(jnp.maximum(jnp.max(s, axis=-1, keepdims=True), jnp.max(sn, axis=-1, keepdims=True)), sink)
        p = jnp.exp(s - m)
        pn = jnp.exp(sn - m)
        l = jnp.sum(p, axis=-1, keepdims=True) + jnp.sum(pn, axis=-1, keepdims=True) + jnp.exp(sink - m)
        o = lax.dot_general(p.astype(BF16), vt, (((1,), (1,)), ((), ())), preferred_element_type=F32)
        o = (o + jnp.dot(pn.astype(BF16), _pad_rows(vnt[:, ksl]), preferred_element_type=F32)) / l
        for g, h in enumerate(heads):
            o_scr[:, h * HEAD_DIM:(h + 1) * HEAD_DIM] = o[g * SUBLANES:(g + 1) * SUBLANES, :]


def _step_attend_b(c_ref, q, knt, vnt, bc_ref, bn_ref, head_lanes):
    kw = B_OUT_W
    qx = jnp.concatenate([jnp.where(mk, q, 0.0) for mk in head_lanes], axis=0).astype(BF16)
    kt = c_ref[0, 0:kw, :].astype(BF16)
    vt = c_ref[0, kw:2 * kw, :].astype(BF16)
    s = jnp.dot(qx, kt, preferred_element_type=F32) + bc_ref[...]
    sn = lax.dot_general(qx, _pad_rows(knt), (((1,), (1,)), ((), ())), preferred_element_type=F32) + bn_ref[...]
    m = jnp.maximum(jnp.max(s, axis=-1, keepdims=True), jnp.max(sn, axis=-1, keepdims=True))
    p = jnp.exp(s - m)
    pn = jnp.exp(sn - m)
    l = jnp.sum(p, axis=-1, keepdims=True) + jnp.sum(pn, axis=-1, keepdims=True)
    ox = lax.dot_general(p.astype(BF16), vt, (((1,), (1,)), ((), ())), preferred_element_type=F32)
    ox = ox + jnp.dot(pn.astype(BF16), _pad_rows(vnt), preferred_element_type=F32)
    o = jnp.zeros((SUBLANES, kw), F32)
    lrow = jnp.ones((SUBLANES, kw), F32)
    mrow = jnp.zeros((SUBLANES, kw), F32)
    for h, mk in enumerate(head_lanes):
        rows = slice(h * SUBLANES, (h + 1) * SUBLANES)
        o = jnp.where(mk, ox[rows], o)
        lrow = jnp.where(mk, l[rows], lrow)
        mrow = jnp.where(mk, m[rows], mrow)
    return o / lrow, mrow + jnp.log(lrow)


def _step_kernel(ca_ref, c1_ref, c2_ref, c3_ref, hq_ref, fa_ref, fb_ref,
                 bca_ref, bna_ref, sink_ref, bc1_ref, bn1_ref, bc2_ref, bn2_ref, bc3_ref, bn3_ref,
                 na_ref, n1_ref, n2_ref, n3_ref, oa_ref, ob_ref, oa_scr):
    b = pl.program_id(0)
    lo = DEC_T * lax.rem(b, LANES // DEC_T)

    _shift_window(ca_ref, na_ref, fa_ref[0:A_KV_W, :], fa_ref[A_KV_W:2 * A_KV_W, :], lo)
    groups = ((c1_ref, n1_ref, bc1_ref, bn1_ref), (c2_ref, n2_ref, bc2_ref, bn2_ref), (c3_ref, n3_ref, bc3_ref, bn3_ref))
    n_groups = len(groups)
    for g, (c_ref, n_ref, _, _) in enumerate(groups):
        _shift_window(c_ref, n_ref, fb_ref[g * B_OUT_W:(g + 1) * B_OUT_W, :],
                      fb_ref[(n_groups + g) * B_OUT_W:(n_groups + g + 1) * B_OUT_W, :], lo)

    _step_attend_a(ca_ref, hq_ref[:, C_QA:C_QA + A_OUT_W], hq_ref[:, C_KA:C_KA + A_KV_W], hq_ref[:, C_VA:C_VA + A_KV_W],
                   bca_ref, bna_ref, sink_ref, oa_scr)

    lane = lax.broadcasted_iota(jnp.int32, (1, B_OUT_W), 1)
    head_lanes = [lane // HEAD_DIM == h for h in range(B_HEADS)]
    outs, lses = [], []
    for g, (c_ref, _, bc_ref, bn_ref) in enumerate(groups):
        sl = lambda c0: slice(c0 + g * B_OUT_W, c0 + (g + 1) * B_OUT_W)
        o, lse = _step_attend_b(c_ref, hq_ref[:, sl(C_QB)], hq_ref[:, sl(C_KB)], hq_ref[:, sl(C_VB)], bc_ref, bn_ref,
                                head_lanes)
        outs.append(o)
        lses.append(lse)
    mx = functools.reduce(jnp.maximum, lses)
    es = [jnp.exp(lse - mx) for lse in lses]
    ob = sum(e * o for e, o in zip(es, outs)) / sum(es)

    half = lax.rem(b, 2)
    mine = lax.broadcasted_iota(jnp.int32, (SUBLANES, 1), 0) // DEC_T == half

    @pl.when(half == 0)
    def _():
        oa_ref[...] = jnp.where(mine, oa_scr[...], 0.0)
        ob_ref[...] = jnp.where(mine, ob, 0.0)

    @pl.when(half == 1)
    def _():
        oa_ref[...] = jnp.where(mine, oa_scr[...], oa_ref[...])
        ob_ref[...] = jnp.where(mine, ob, ob_ref[...])


def _step_attn(caches, hs, ht, tables):
    db = caches[0].shape[0]
    per_tile = LANES // DEC_T
    per_blk = SUBLANES // DEC_T
    n_kb = len(B_PATTERNS) * B_OUT_W
    cache_specs = [pl.BlockSpec((1,) + c.shape[1:], lambda b: (b, 0, 0)) for c in caches]
    in_specs = cache_specs + [
        pl.BlockSpec((SUBLANES, QKV_W), lambda b: (b // per_blk, 0)),
        pl.BlockSpec((2 * A_KV_W, LANES), lambda b: (C_KA // (2 * A_KV_W), b // per_tile)),
        pl.BlockSpec((2 * n_kb, LANES), lambda b: (C_KB // (2 * n_kb), b // per_tile)),
    ] + [pl.BlockSpec(t.shape, lambda b, nd=t.ndim: (0,) * nd) for t in tables]
    o_specs = [pl.BlockSpec((SUBLANES, A_OUT_W), lambda b: (b // per_blk, 0)),
               pl.BlockSpec((SUBLANES, B_OUT_W), lambda b: (b // per_blk, 0))]
    return pl.pallas_call(
        _step_kernel,
        grid=(db,),
        in_specs=in_specs,
        out_specs=tuple(cache_specs + o_specs),
        out_shape=tuple([jax.ShapeDtypeStruct(c.shape, F32) for c in caches]
                        + [jax.ShapeDtypeStruct((db * DEC_T, A_OUT_W), F32), jax.ShapeDtypeStruct((db * DEC_T, B_OUT_W), F32)]),
        scratch_shapes=[pltpu.VMEM((SUBLANES, A_OUT_W), F32)],
        compiler_params=_cparams(("arbitrary",)),
        name="step_attn",
    )(*caches, hs, ht, ht, *tables)


PREP_W = C_QB - C_PAD


def _prep_w_in_kernel(w_ref, o_ref):
    j = pl.program_id(0)
    is_q = (j < C_KA // PREP_W) | ((j >= C_QB // PREP_W) & (j < C_KB // PREP_W))
    scale = jnp.where(is_q, SCALE, 1.0)
    o_ref[...] = jnp.where(j == C_PAD // PREP_W, 0.0, w_ref[...] * scale).astype(o_ref.dtype)


def _prep_w_in(w):
    pad_blk = C_PAD // PREP_W
    return pl.pallas_call(
        _prep_w_in_kernel,
        grid=(HW // PREP_W,),
        in_specs=[pl.BlockSpec((w.shape[0], PREP_W), lambda j: (0, jnp.where(j > pad_blk, j - 1, jnp.minimum(j, pad_blk - 1))))],
        out_specs=pl.BlockSpec((w.shape[0], PREP_W), lambda j: (0, j)),
        out_shape=jax.ShapeDtypeStruct((w.shape[0], HW), BF16),
        compiler_params=_cparams(("parallel",)),
        name="prep_w_in",
    )(w)


def _to_feature_major(cache):
    db, win = cache.shape[:2]
    return jnp.transpose(cache, (0, 2, 3, 4, 1)).reshape(db, -1, win)


def _from_feature_major(ct, heads):
    n, _, win = ct.shape
    return jnp.transpose(ct.reshape(n, 2, heads, HEAD_DIM, win), (0, 4, 1, 2, 3))[None]


def _tail_layers(x, h, oa, ob, w_oa, w_ob, w_out, ln1_g, ln1_b, w_ffn_in, w_ffn_out, ln2_g, ln2_b):
    t = x.shape[0]
    tm = min(1024, t)
    mixin = _gate_proj(oa, ob, h, w_oa, w_ob, min(512, t), D_MODEL)
    h1, h1b = _mm_res_ln(mixin, w_out, x, ln1_g, ln1_b, min(512, t), D_MODEL, "out_ln1", also_bf16=True)
    u = _ffn_in(h1b, w_ffn_in, min(2048, t), 512)
    return _mm_res_ln(u, w_ffn_out, h1, ln2_g, ln2_b, tm, D_FF // 4, "ffn_out_ln2")


def kernel(x_prompt, x_sample, cache_a, cache_b1, cache_b2, cache_b3, rel_bias, w_in, a_sink, w_oa, w_ob,
           w_out, ln1_g, ln1_b, w_ffn_in, w_ffn_out, ln2_g, ln2_b):
    batch, seq, _ = x_prompt.shape
    db, dt, _ = x_sample.shape
    assert dt == DEC_T and w_in.shape[0] == DEPTH and db % (SUBLANES // DEC_T) == 0
    tp, ts = batch * seq, db * dt

    w_in_b = _prep_w_in(w_in[0])
    weights = (w_oa[0].astype(BF16), w_ob[0].astype(BF16), w_out[0].astype(BF16), ln1_g, ln1_b,
               w_ffn_in[0], w_ffn_out[0].astype(BF16), ln2_g, ln2_b)
    sink = a_sink[0].astype(F32)
    b_h0 = [A_Q_HEADS + g * B_HEADS for g in range(len(B_PATTERNS))]
    b_cols = [(C_QB + g * B_OUT_W, C_KB + g * B_OUT_W, C_VB + g * B_OUT_W) for g in range(len(B_PATTERNS))]

    xp = x_prompt.reshape(tp, D_MODEL)
    hp = _in_proj(xp, w_in_b, 1024, 2048)
    bias_a = _packed_rows(_band_bias(rel_bias, A_WINDOW - 1, 1, 0, A_Q_HEADS), A_Q_HEADS // 2, 1)
    oa = _band_attn_a(hp, bias_a, sink, batch=batch, seq=seq, nblk=4)
    others, ob = [], None
    for g, (win, dil) in sorted(enumerate(B_PATTERNS), key=lambda e: -e[1][1]):
        cq, ck, cv = b_cols[g]
        bias_g = _packed_rows(_band_bias(rel_bias, win // dil, dil, b_h0[g], B_HEADS), B_HEADS // 2, 1)
        res = _dil_attn(hp, bias_g, batch=batch, seq=seq, dil=dil, nblk={1: 16, 4: 4, 16: 1}[dil], cq=cq, ck=ck, cv=cv,
                        merge=others if dil == 1 else ())
        if dil == 1:
            ob = res
        else:
            others.append(res)
    yp = _tail_layers(xp, hp, oa, ob, *weights).reshape(batch, seq, D_MODEL)

    new_a_p = _from_feature_major(_kv_tail(hp, batch=batch, seq=seq, win=min(A_WINDOW, seq), cw=A_KV_W, ck=C_KA, cv=C_VA),
                                  A_KV_HEADS)
    new_b_p = [_from_feature_major(_kv_tail(hp, batch=batch, seq=seq, win=min(win, seq), cw=B_OUT_W,
                                            ck=b_cols[g][1], cv=b_cols[g][2]), B_HEADS)
               for g, (win, dil) in enumerate(B_PATTERNS)]

    xs = x_sample.reshape(ts, D_MODEL)
    hs = _in_proj(xs, w_in_b, ts, 1024)
    lanes = -(-ts // LANES) * LANES
    ht = jnp.pad(hs[:, :QKV_W].T, ((0, 0), (0, lanes - ts)))

    group_a = A_Q_HEADS // A_KV_HEADS
    sink_rows = jnp.broadcast_to(jnp.repeat(sink.reshape(A_KV_HEADS, group_a), SUBLANES, axis=1)[:, :, None],
                                 (A_KV_HEADS, group_a * SUBLANES, LANES))
    tables = list(_step_bias(rel_bias, cache_a.shape[2], 1, 0, A_Q_HEADS, A_KV_HEADS, True)) + [sink_rows]
    for g, ((win, dil), cache) in enumerate(zip(B_PATTERNS, (cache_b1, cache_b2, cache_b3))):
        bc, bn = _step_bias(rel_bias, cache.shape[2], dil, b_h0[g], B_HEADS, 1, False)
        tables += [bc[0], bn[0]]
    caches = [_to_feature_major(c[0]) for c in (cache_a, cache_b1, cache_b2, cache_b3)]
    new_a_t, new_b1_t, new_b2_t, new_b3_t, oa_s, ob_s = _step_attn(caches, hs, ht, tables)
    ys = _tail_layers(xs, hs, oa_s.astype(BF16), ob_s.astype(BF16), *weights).reshape(db, dt, D_MODEL)

    return (yp, ys, new_a_p, new_b_p[0], new_b_p[1], new_b_p[2],
            _from_feature_major(new_a_t, A_KV_HEADS), _from_feature_major(new_b1_t, B_HEADS),
            _from_feature_major(new_b2_t, B_HEADS), _from_feature_major(new_b3_t, B_HEADS))
```

```python
import functools
import math

import numpy as np
import jax
import jax.numpy as jnp
from jax import lax
from jax.experimental import pallas as pl
from jax.experimental.pallas import tpu as pltpu

F32 = jnp.float32
BF16 = jnp.bfloat16

D_MODEL = 2048
HEAD_DIM = 64
A_WINDOW = 128
A_Q_HEADS = 16
A_KV_HEADS = 2
B_PATTERNS = ((128, 1), (512, 4), (2048, 16))
B_HEADS = 8
NUM_BUCKETS = 32
REL_MAX_DIST = 2048
BLOCK = 128
D_FF = 5632
DEPTH = 1
ALPHA = (2 * DEPTH) ** 0.25
SCALE = HEAD_DIM ** -0.5
LN_EPS = 1e-5
NEG = -1e30
LANES = 128
SUBLANES = 8
DEC_T = 4

A_OUT_W = A_Q_HEADS * HEAD_DIM
A_KV_W = A_KV_HEADS * HEAD_DIM
B_OUT_W = B_HEADS * HEAD_DIM
C_QA, C_KA, C_VA, C_PAD, C_QB, C_KB, C_VB, C_GA, C_GB = 0, 1024, 1152, 1280, 1536, 3072, 4608, 6144, 8192
HW = 10240
QKV_W = C_GA
VMEM_LIMIT = 56 * 1024 * 1024


def _cparams(sem):
    return pltpu.CompilerParams(dimension_semantics=sem, vmem_limit_bytes=VMEM_LIMIT)


def _bucket_np(dist):
    d = np.maximum(np.asarray(dist, np.int64), 0)
    max_exact = NUM_BUCKETS // 2
    ratio = np.maximum(d, max_exact).astype(np.float32) / np.float32(max_exact)
    large = max_exact + (np.log(ratio) / np.float32(math.log(REL_MAX_DIST / max_exact))
                         * np.float32(NUM_BUCKETS - max_exact)).astype(np.int32)
    return np.where(d < max_exact, d, np.minimum(large, NUM_BUCKETS - 1)).astype(np.int32)


def _dist_table(rel_bias):
    return jnp.take(rel_bias, jnp.asarray(_bucket_np(np.arange(REL_MAX_DIST + 1))), axis=0).astype(F32)


def _bias_by_dist(table, dists, h0, nh):
    d = np.asarray(dists)
    step = int(d[1] - d[0])
    assert step != 0 and np.all(np.diff(d) == step)
    lo, hi = (int(d[0]), int(d[-1])) if step > 0 else (int(d[-1]), int(d[0]))
    rows = lax.slice(table, (lo, h0), (hi + 1, h0 + nh), (abs(step), 1))
    return (rows if step > 0 else rows[::-1]).T


def _band_bias(table, max_dist, dil, h0, nh):
    u = BLOCK - np.arange(2 * BLOCK)
    near_side = _bias_by_dist(table, np.arange(BLOCK, -1, -1) * dil, h0, nh)
    w = jnp.concatenate([near_side, jnp.full((nh, BLOCK - 1), NEG, F32)], axis=1)
    w = jnp.where(jnp.asarray((u >= 0) & (u <= max_dist))[None], w, NEG)
    x = jnp.concatenate([w, jnp.full((nh, 1), NEG, F32)], axis=1)
    band = jnp.tile(x, (1, BLOCK))[:, :BLOCK * 2 * BLOCK].reshape(nh, BLOCK, 2 * BLOCK)
    return jnp.stack([band, jnp.where(jnp.asarray(np.arange(2 * BLOCK) < BLOCK), NEG, band)])


def _packed_rows(per_head, groups, pairs):
    lead, tail = per_head.shape[:-3], per_head.shape[-1]
    t = per_head.reshape(*lead, groups, pairs, 2, BLOCK, tail)
    t = jnp.swapaxes(t, -4, -3)
    return t.reshape(*lead, groups, 2 * pairs * BLOCK, tail)


def _step_bias(table, win, dil, h0, nh, n_kv, is_a):
    neg = lambda n: jnp.full((nh, n), NEG, F32)
    rows_c = []
    if is_a:
        rev = _bias_by_dist(table, np.arange(A_WINDOW - 1, -1, -1), h0, nh)
        for t in range(DEC_T):
            rows_c.append(jnp.concatenate([neg(t + 1), rev[:, :win - t - 1]], axis=1))
    elif dil == 1:
        rev = _bias_by_dist(table, np.arange(win, 0, -1), h0, nh)
        for t in range(DEC_T):
            rows_c.append(jnp.concatenate([neg(t), rev[:, :win - t]], axis=1))
    else:
        rev = _bias_by_dist(table, np.arange(win // dil, 0, -1) * dil, h0, nh)
        for t in range(DEC_T):
            slots = [rev[:, :, None] if r == t else jnp.full((nh, win // dil, 1), NEG, F32) for r in range(dil)]
            rows_c.append(jnp.concatenate(slots, axis=2).reshape(nh, win))
    bc = jnp.stack(rows_c * 2, axis=1)

    i = np.arange(SUBLANES)[:, None]
    j = np.arange(LANES)[None, :]
    dn = i % DEC_T - j % DEC_T
    vn = (j < SUBLANES) & (i // DEC_T == j // DEC_T) & (dn >= 0)
    if not is_a:
        vn &= dn % dil == 0
    near = _bias_by_dist(table, np.arange(DEC_T), h0, nh)
    bn = jnp.full((nh, SUBLANES, LANES), NEG, F32)
    for d in range(DEC_T):
        bn = jnp.where(jnp.asarray(vn & (dn == d))[None], near[:, d][:, None, None], bn)
    g8 = (nh // n_kv) * SUBLANES
    return bc.reshape(n_kv, g8, win), bn.reshape(n_kv, g8, LANES)


def _inproj_kernel(x_ref, w_ref, o_ref, xb_ref):
    @pl.when(pl.program_id(1) == 0)
    def _():
        xb_ref[...] = x_ref[...].astype(BF16)

    o_ref[...] = jnp.dot(xb_ref[...], w_ref[...], preferred_element_type=F32)


def _in_proj(x, w, tm, tn):
    t, k = x.shape
    n = w.shape[1]
    return pl.pallas_call(
        _inproj_kernel,
        grid=(t // tm, n // tn),
        in_specs=[pl.BlockSpec((tm, k), lambda i, j: (i, 0)),
                  pl.BlockSpec((k, tn), lambda i, j: (0, j))],
        out_specs=pl.BlockSpec((tm, tn), lambda i, j: (i, j)),
        out_shape=jax.ShapeDtypeStruct((t, n), F32),
        scratch_shapes=[pltpu.VMEM((tm, k), BF16)],
        compiler_params=_cparams(("parallel", "arbitrary")),
        name="in_proj",
    )(x, w)


def _gate_proj_kernel(oa_ref, ob_ref, ga_ref, gb_ref, woa_ref, wob_ref, o_ref):
    pa = jnp.dot(oa_ref[...], woa_ref[...], preferred_element_type=F32)
    pb = jnp.dot(ob_ref[...], wob_ref[...], preferred_element_type=F32)
    sa = 0.5 * jnp.tanh(0.5 * ga_ref[...]) + 0.5
    sb = 0.5 * jnp.tanh(0.5 * gb_ref[...]) + 0.5
    o_ref[...] = (sa * pa + sb * pb).astype(o_ref.dtype)


def _gate_proj(oa, ob, h, w_oa, w_ob, tm, tn):
    t = oa.shape[0]
    nj = D_MODEL // tn
    return pl.pallas_call(
        _gate_proj_kernel,
        grid=(t // tm, nj),
        in_specs=[pl.BlockSpec((tm, A_OUT_W), lambda i, j: (i, 0)),
                  pl.BlockSpec((tm, B_OUT_W), lambda i, j: (i, 0)),
                  pl.BlockSpec((tm, tn), lambda i, j: (i, C_GA // tn + j)),
                  pl.BlockSpec((tm, tn), lambda i, j: (i, C_GB // tn + j)),
                  pl.BlockSpec((A_OUT_W, tn), lambda i, j: (0, j)),
                  pl.BlockSpec((B_OUT_W, tn), lambda i, j: (0, j))],
        out_specs=pl.BlockSpec((tm, tn), lambda i, j: (i, j)),
        out_shape=jax.ShapeDtypeStruct((t, D_MODEL), BF16),
        compiler_params=_cparams(("parallel", "arbitrary")),
        name="gate_proj",
    )(oa, ob, h, h, w_oa, w_ob)


def _layer_norm(z, g, b):
    mu = jnp.mean(z, axis=-1, keepdims=True)
    zc = z - mu
    var = jnp.mean(zc * zc, axis=-1, keepdims=True)
    return zc * lax.rsqrt(var + LN_EPS) * g + b


def _mm_res_ln_kernel(a_ref, w_ref, r_ref, g_ref, b_ref, o_ref, ob_ref=None, *, n_chunk):
    k = pl.program_id(1)
    nk = pl.num_programs(1)
    if n_chunk is None:
        sub = min(128, o_ref.shape[0])
        for r0 in range(0, o_ref.shape[0], sub):
            rows = slice(r0, r0 + sub)
            mix = jnp.dot(a_ref[rows, :], w_ref[...], preferred_element_type=F32)
            y = _layer_norm(ALPHA * r_ref[rows, :] + mix, g_ref[...], b_ref[...])
            o_ref[rows, :] = y
            if ob_ref is not None:
                ob_ref[rows, :] = y.astype(ob_ref.dtype)
        return

    def accumulate(first):
        a = a_ref[...]
        for c in range(D_MODEL // n_chunk):
            cols = slice(c * n_chunk, (c + 1) * n_chunk)
            part = jnp.dot(a, w_ref[:, cols], preferred_element_type=F32)
            o_ref[:, cols] = part if first else o_ref[:, cols] + part

    pl.when(k == 0)(lambda: accumulate(True))
    pl.when(k > 0)(lambda: accumulate(False))

    @pl.when(k == nk - 1)
    def _():
        rows_per_pass = 256
        for r0 in range(0, o_ref.shape[0], rows_per_pass):
            rows = slice(r0, r0 + rows_per_pass)
            o_ref[rows, :] = _layer_norm(ALPHA * r_ref[rows, :] + o_ref[rows, :], g_ref[...], b_ref[...])


def _mm_res_ln(a, w, res, g, b, tm, tk, name, also_bf16=False):
    t, kdim = a.shape
    assert not also_bf16 or tk == kdim
    o_spec = pl.BlockSpec((tm, D_MODEL), lambda i, k: (i, 0))
    o_shape = jax.ShapeDtypeStruct((t, D_MODEL), F32)
    return pl.pallas_call(
        functools.partial(_mm_res_ln_kernel, n_chunk=None if tk == kdim else 512),
        grid=(t // tm, kdim // tk),
        in_specs=[pl.BlockSpec((tm, tk), lambda i, k: (i, k)),
                  pl.BlockSpec((tk, D_MODEL), lambda i, k: (k, 0)),
                  pl.BlockSpec((tm, D_MODEL), lambda i, k: (i, 0)),
                  pl.BlockSpec((1, D_MODEL), lambda i, k: (0, 0)),
                  pl.BlockSpec((1, D_MODEL), lambda i, k: (0, 0))],
        out_specs=(o_spec, o_spec) if also_bf16 else o_spec,
        out_shape=(o_shape, jax.ShapeDtypeStruct((t, D_MODEL), BF16)) if also_bf16 else o_shape,
        compiler_params=_cparams(("parallel", "arbitrary")),
        name=name,
    )(a, w, res, g, b)


def _ffn_in_kernel(h_ref, wg_ref, wu_ref, o_ref, *, sub):
    wg = wg_ref[...].astype(BF16)
    wu = wu_ref[...].astype(BF16)
    for r0 in range(0, o_ref.shape[0], sub):
        rows = slice(r0, r0 + sub)
        hb = h_ref[rows, :]
        gate = jnp.dot(hb, wg, preferred_element_type=F32)
        up = jnp.dot(hb, wu, preferred_element_type=F32)
        o_ref[rows, :] = (gate * jax.nn.sigmoid(gate) * up).astype(o_ref.dtype)


def _ffn_in(h1b, w_ffn_in, tm, tn):
    t = h1b.shape[0]
    nj = D_FF // tn
    return pl.pallas_call(
        functools.partial(_ffn_in_kernel, sub=min(1024, tm)),
        grid=(t // tm, nj),
        in_specs=[pl.BlockSpec((tm, D_MODEL), lambda i, j: (i, 0)),
                  pl.BlockSpec((D_MODEL, tn), lambda i, j: (0, j)),
                  pl.BlockSpec((D_MODEL, tn), lambda i, j: (0, nj + j))],
        out_specs=pl.BlockSpec((tm, tn), lambda i, j: (i, j)),
        out_shape=jax.ShapeDtypeStruct((t, D_FF), BF16),
        compiler_params=_cparams(("parallel", "arbitrary")),
        name="ffn_in",
    )(h1b, w_ffn_in, w_ffn_in)


def _head_lane_masks():
    lane = lax.broadcasted_iota(jnp.int32, (1, LANES), 1)
    return [lane < HEAD_DIM, lane >= HEAD_DIM]


def _packed_attention(q, k2b, v2, bias, masks, sinks):
    q2 = jnp.concatenate([jnp.where(mk, q, 0.0) for mk in masks], axis=0).astype(BF16)
    s = lax.dot_general(q2, k2b, (((1,), (1,)), ((), ())), preferred_element_type=F32) + bias
    m = jnp.max(s, axis=-1, keepdims=True)
    if sinks is not None:
        sink = jnp.where(lax.broadcasted_iota(jnp.int32, (2 * BLOCK, 1), 0) < BLOCK, sinks[0], sinks[1])
        m = jnp.maximum(m, sink)
    p = jnp.exp(s - m).astype(BF16)
    nd = jnp.zeros((BLOCK, 2 * LANES), F32)
    for hh, mk in enumerate(masks):
        ones = jnp.broadcast_to(jnp.where(mk, 1.0, 0.0), v2.shape)
        w = jnp.concatenate([jnp.where(mk, v2, 0.0), ones], axis=1).astype(BF16)
        nd = nd + jnp.dot(p[hh * BLOCK:(hh + 1) * BLOCK], w, preferred_element_type=F32)
    den = nd[:, LANES:]
    if sinks is not None:
        es = jnp.exp(sink - m)
        den = den + jnp.where(masks[0], es[:BLOCK], es[BLOCK:])
    return nd[:, :LANES] / den, jnp.where(masks[0], m[:BLOCK], m[BLOCK:]) + jnp.log(den)


def _band_attn_kernel(q_ref, kc_ref, kp_ref, vc_ref, vp_ref, bias_ref, sink_ref, o_ref):
    first = (pl.program_id(1) == 0).astype(jnp.int32)
    masks = _head_lane_masks()
    pairs_per_kv = (A_Q_HEADS // A_KV_HEADS) // 2
    for j in range(q_ref.shape[0] // BLOCK):
        rows = slice(j * BLOCK, (j + 1) * BLOCK)
        if j == 0:
            kp, vp, dead = kp_ref[...], vp_ref[...], first
        else:
            kp, vp, dead = kc_ref[(j - 1) * BLOCK:j * BLOCK, :], vc_ref[(j - 1) * BLOCK:j * BLOCK, :], 0
        kx = jnp.concatenate([kp, kc_ref[rows, :]], axis=0)
        vx = jnp.concatenate([vp, vc_ref[rows, :]], axis=0)
        kr = pltpu.roll(kx, HEAD_DIM, 1)
        vr = pltpu.roll(vx, HEAD_DIM, 1)
        for kv in range(A_KV_HEADS):
            k2 = jnp.where(masks[kv], kx, kr).astype(BF16)
            v2 = jnp.where(masks[kv], vx, vr)
            for pp in range(pairs_per_kv):
                pi = kv * pairs_per_kv + pp
                sl = slice(pi * LANES, (pi + 1) * LANES)
                out, _ = _packed_attention(q_ref[rows, sl], k2, v2, bias_ref[dead, pi], masks,
                                           (sink_ref[2 * pi], sink_ref[2 * pi + 1]))
                o_ref[rows, sl] = out.astype(o_ref.dtype)


def _band_attn_a(h, bias, sink, *, batch, seq, nblk):
    chunk = BLOCK * nblk
    nc = seq // chunk
    qw, kw = A_OUT_W, A_KV_W

    def cur(b, i):
        return b * nc + i

    def prev(b, i):
        return jnp.maximum((b * nc + i) * nblk - 1, 0)

    return pl.pallas_call(
        _band_attn_kernel,
        grid=(batch, nc),
        in_specs=[pl.BlockSpec((chunk, qw), lambda b, i: (cur(b, i), C_QA // qw)),
                  pl.BlockSpec((chunk, kw), lambda b, i: (cur(b, i), C_KA // kw)),
                  pl.BlockSpec((BLOCK, kw), lambda b, i: (prev(b, i), C_KA // kw)),
                  pl.BlockSpec((chunk, kw), lambda b, i: (cur(b, i), C_VA // kw)),
                  pl.BlockSpec((BLOCK, kw), lambda b, i: (prev(b, i), C_VA // kw)),
                  pl.BlockSpec(bias.shape, lambda b, i: (0, 0, 0, 0)),
                  pl.BlockSpec(memory_space=pltpu.SMEM)],
        out_specs=pl.BlockSpec((chunk, qw), lambda b, i: (cur(b, i), 0)),
        out_shape=jax.ShapeDtypeStruct((batch * seq, qw), BF16),
        compiler_params=_cparams(("parallel", "arbitrary")),
        name="band_attn_a",
    )(h, h, h, h, h, bias, sink)


def _dil_attn_kernel(*refs, dil, nblk, unroll, has_prev, n_merge):
    refs = list(refs)
    q_ref, kc_ref, vc_ref = refs[:3]
    del refs[:3]
    if has_prev:
        kp_ref, vp_ref = refs[:2]
        del refs[:2]
    bias_ref = refs.pop(0)
    others = []
    if n_merge:
        others = [(refs[2 * g], refs[2 * g + 1]) for g in range(n_merge)]
        del refs[:2 * n_merge]
        (o_ref,) = refs
    else:
        o_ref, lse_ref = refs
    first = (pl.program_id(2) == 0).astype(jnp.int32)
    masks = _head_lane_masks()

    def rows(r, j):
        return pl.ds(j * BLOCK * dil + r, BLOCK, stride=dil) if dil > 1 else pl.ds(j * BLOCK, BLOCK)

    def tile(r, j):
        cur = rows(r, j)
        if not has_prev:
            k2, v2, bias = kc_ref[cur, :].astype(BF16), vc_ref[cur, :], bias_ref[0, 0][:, BLOCK:]
        else:
            if j == 0:
                kp, vp, dead = kp_ref[rows(r, 0), :], vp_ref[rows(r, 0), :], first
            else:
                kp, vp, dead = kc_ref[rows(r, j - 1), :], vc_ref[rows(r, j - 1), :], 0
            k2 = jnp.concatenate([kp, kc_ref[cur, :]], axis=0).astype(BF16)
            v2 = jnp.concatenate([vp, vc_ref[cur, :]], axis=0)
            bias = bias_ref[dead, 0]
        out, lse = _packed_attention(q_ref[cur, :], k2, v2, bias, masks, None)
        if not n_merge:
            o_ref[cur, :] = out
            lse_ref[cur, :] = lse
            return
        lses = [lse] + [l_ref[cur, :] for _, l_ref in others]
        outs = [out] + [og_ref[cur, :] for og_ref, _ in others]
        mx = functools.reduce(jnp.maximum, lses)
        es = [jnp.exp(l - mx) for l in lses]
        o_ref[cur, :] = (sum(e * o for e, o in zip(es, outs)) / sum(es)).astype(o_ref.dtype)

    def body(it, carry):
        for u in range(unroll):
            for j in range(nblk):
                tile(it * unroll + u, j)
        return carry

    if dil == unroll:
        body(0, 0)
    else:
        lax.fori_loop(0, dil // unroll, body, 0)


def _dil_attn(h, bias, *, batch, seq, dil, nblk, cq, ck, cv, merge=()):
    band = BLOCK * dil
    chunk = band * nblk
    nc = seq // chunk
    pairs = B_OUT_W // LANES

    def cur(b, p, i):
        return b * nc + i

    def prev(b, p, i):
        return jnp.maximum((b * nc + i) * nblk - 1, 0)

    o_spec = pl.BlockSpec((chunk, LANES), lambda b, p, i: (cur(b, p, i), p))
    o_shape = jax.ShapeDtypeStruct((batch * seq, B_OUT_W), F32)
    has_prev = seq > band
    in_specs = [pl.BlockSpec((chunk, LANES), lambda b, p, i: (cur(b, p, i), cq // LANES + p)),
                pl.BlockSpec((chunk, LANES), lambda b, p, i: (cur(b, p, i), ck // LANES + p)),
                pl.BlockSpec((chunk, LANES), lambda b, p, i: (cur(b, p, i), cv // LANES + p))]
    if has_prev:
        in_specs += [pl.BlockSpec((band, LANES), lambda b, p, i: (prev(b, p, i), ck // LANES + p)),
                     pl.BlockSpec((band, LANES), lambda b, p, i: (prev(b, p, i), cv // LANES + p))]
    n_h = len(in_specs)
    in_specs.append(pl.BlockSpec((2, 1, 2 * BLOCK, 2 * BLOCK), lambda b, p, i: (0, p, 0, 0)))
    others = [a for pair in merge for a in pair]
    in_specs += [o_spec] * len(others)
    if merge:
        assert dil == 1, "the bf16 output needs unstrided stores"
        out_specs, out_shape = o_spec, jax.ShapeDtypeStruct((batch * seq, B_OUT_W), BF16)
    else:
        out_specs, out_shape = (o_spec, o_spec), (o_shape, o_shape)
    return pl.pallas_call(
        functools.partial(_dil_attn_kernel, dil=dil, nblk=nblk, unroll=max(1, min(dil, 8 // nblk)), has_prev=has_prev,
                          n_merge=len(merge)),
        grid=(batch, pairs, nc),
        in_specs=in_specs,
        out_specs=out_specs,
        out_shape=out_shape,
        compiler_params=_cparams(("parallel", "parallel", "arbitrary")),
        name=f"dil_attn_d{dil}",
    )(*([h] * n_h), bias, *others)


def _kv_tail_kernel(k_ref, v_ref, o_ref):
    cw = k_ref.shape[1]
    o_ref[0, 0:cw, :] = k_ref[...].T
    o_ref[0, cw:2 * cw, :] = v_ref[...].T


def _kv_tail(h, *, batch, seq, win, cw, ck, cv):
    rows = min(win, 8 * BLOCK)
    nblk = win // rows
    base = (seq - win) // rows
    per = seq // rows
    return pl.pallas_call(
        _kv_tail_kernel,
        grid=(batch, nblk),
        in_specs=[pl.BlockSpec((rows, cw), lambda b, i: (b * per + base + i, ck // cw)),
                  pl.BlockSpec((rows, cw), lambda b, i: (b * per + base + i, cv // cw))],
        out_specs=pl.BlockSpec((1, 2 * cw, rows), lambda b, i: (b, 0, i)),
        out_shape=jax.ShapeDtypeStruct((batch, 2 * cw, win), F32),
        compiler_params=_cparams(("parallel", "parallel")),
        name=f"kv_tail_w{win}_c{cw}",
    )(h, h)


def _shift_window(c_ref, cout_ref, knf, vnf, lo):
    lane = lax.broadcasted_iota(jnp.int32, (1, LANES), 1)
    keep = lane < LANES - DEC_T
    to_tail = lax.rem(2 * LANES - DEC_T - lo, LANES)
    new_tail = jnp.concatenate([pltpu.roll(knf, to_tail, 1), pltpu.roll(vnf, to_tail, 1)], axis=0)
    nlb = c_ref.shape[2] // LANES
    nxt = pltpu.roll(c_ref[0, :, 0:LANES], LANES - DEC_T, 1)
    for j in range(nlb):
        cur = nxt
        if j + 1 < nlb:
            nxt = pltpu.roll(c_ref[0, :, (j + 1) * LANES:(j + 2) * LANES], LANES - DEC_T, 1)
        else:
            nxt = new_tail
        cout_ref[0, :, j * LANES:(j + 1) * LANES] = jnp.where(keep, cur, nxt)


def _pad_rows(x):
    return jnp.concatenate([x, jnp.zeros((LANES - SUBLANES, x.shape[1]), F32)], axis=0).astype(BF16)


def _step_attend_a(c_ref, q, knt, vnt, bc_ref, bn_ref, sink_ref, o_scr):
    kvw = A_KV_W
    group = A_Q_HEADS // A_KV_HEADS
    for kv in range(A_KV_HEADS):
        ksl = slice(kv * HEAD_DIM, (kv + 1) * HEAD_DIM)
        kt = c_ref[0, ksl, :].astype(BF16)
        vt = c_ref[0, kvw + kv * HEAD_DIM:kvw + (kv + 1) * HEAD_DIM, :].astype(BF16)
        heads = range(kv * group, (kv + 1) * group)
        qs = jnp.concatenate([q[:, h * HEAD_DIM:(h + 1) * HEAD_DIM] for h in heads], axis=0).astype(BF16)
        s = jnp.dot(qs, kt, preferred_element_type=F32) + bc_ref[kv]
        sn = lax.dot_general(qs, _pad_rows(knt[:, ksl]), (((1,), (1,)), ((), ())), preferred_element_type=F32) + bn_ref[kv]
        sink = sink_ref[kv][:, 0:1]
        m = jnp.maximum(jnp.maximum(jnp.max(s, axis=-1, keepdims=True), jnp.max(sn, axis=-1, keepdims=True)), sink)
        p = jnp.exp(s - m)
        pn = jnp.exp(sn - m)
        l = jnp.sum(p, axis=-1, keepdims=True) + jnp.sum(pn, axis=-1, keepdims=True) + jnp.exp(sink - m)
        o = lax.dot_general(p.astype(BF16), vt, (((1,), (1,)), ((), ())), preferred_element_type=F32)
        o = (o + jnp.dot(pn.astype(BF16), _pad_rows(vnt[:, ksl]), preferred_element_type=F32)) / l
        for g, h in enumerate(heads):
            o_scr[:, h * HEAD_DIM:(h + 1) * HEAD_DIM] = o[g * SUBLANES:(g + 1) * SUBLANES, :]


def _step_attend_b(c_ref, q, knt, vnt, bc_ref, bn_ref, head_lanes):
    kw = B_OUT_W
    qx = jnp.concatenate([jnp.where(mk, q, 0.0) for mk in head_lanes], axis=0).astype(BF16)
    kt = c_ref[0, 0:kw, :].astype(BF16)
    vt = c_ref[0, kw:2 * kw, :].astype(BF16)
    s = jnp.dot(qx, kt, preferred_element_type=F32) + bc_ref[...]
    sn = lax.dot_general(qx, _pad_rows(knt), (((1,), (1,)), ((), ())), preferred_element_type=F32) + bn_ref[...]
    m = jnp.maximum(jnp.max(s, axis=-1, keepdims=True), jnp.max(sn, axis=-1, keepdims=True))
    p = jnp.exp(s - m)
    pn = jnp.exp(sn - m)
    l = jnp.sum(p, axis=-1, keepdims=True) + jnp.sum(pn, axis=-1, keepdims=True)
    ox = lax.dot_general(p.astype(BF16), vt, (((1,), (1,)), ((), ())), preferred_element_type=F32)
    ox = ox + jnp.dot(pn.astype(BF16), _pad_rows(vnt), preferred_element_type=F32)
    o = jnp.zeros((SUBLANES, kw), F32)
    lrow = jnp.ones((SUBLANES, kw), F32)
    mrow = jnp.zeros((SUBLANES, kw), F32)
    for h, mk in enumerate(head_lanes):
        rows = slice(h * SUBLANES, (h + 1) * SUBLANES)
        o = jnp.where(mk, ox[rows], o)
        lrow = jnp.where(mk, l[rows], lrow)
        mrow = jnp.where(mk, m[rows], mrow)
    return o / lrow, mrow + jnp.log(lrow)


def _step_kernel(ca_ref, c1_ref, c2_ref, c3_ref, hq_ref, fa_ref, fb_ref,
                 bca_ref, bna_ref, sink_ref, bc1_ref, bn1_ref, bc2_ref, bn2_ref, bc3_ref, bn3_ref,
                 na_ref, n1_ref, n2_ref, n3_ref, oa_ref, ob_ref, oa_scr):
    b = pl.program_id(0)
    lo = DEC_T * lax.rem(b, LANES // DEC_T)

    _shift_window(ca_ref, na_ref, fa_ref[0:A_KV_W, :], fa_ref[A_KV_W:2 * A_KV_W, :], lo)
    groups = ((c1_ref, n1_ref, bc1_ref, bn1_ref), (c2_ref, n2_ref, bc2_ref, bn2_ref), (c3_ref, n3_ref, bc3_ref, bn3_ref))
    n_groups = len(groups)
    for g, (c_ref, n_ref, _, _) in enumerate(groups):
        _shift_window(c_ref, n_ref, fb_ref[g * B_OUT_W:(g + 1) * B_OUT_W, :],
                      fb_ref[(n_groups + g) * B_OUT_W:(n_groups + g + 1) * B_OUT_W, :], lo)

    _step_attend_a(ca_ref, hq_ref[:, C_QA:C_QA + A_OUT_W], hq_ref[:, C_KA:C_KA + A_KV_W], hq_ref[:, C_VA:C_VA + A_KV_W],
                   bca_ref, bna_ref, sink_ref, oa_scr)

    lane = lax.broadcasted_iota(jnp.int32, (1, B_OUT_W), 1)
    head_lanes = [lane // HEAD_DIM == h for h in range(B_HEADS)]
    outs, lses = [], []
    for g, (c_ref, _, bc_ref, bn_ref) in enumerate(groups):
        sl = lambda c0: slice(c0 + g * B_OUT_W, c0 + (g + 1) * B_OUT_W)
        o, lse = _step_attend_b(c_ref, hq_ref[:, sl(C_QB)], hq_ref[:, sl(C_KB)], hq_ref[:, sl(C_VB)], bc_ref, bn_ref,
                                head_lanes)
        outs.append(o)
        lses.append(lse)
    mx = functools.reduce(jnp.maximum, lses)
    es = [jnp.exp(lse - mx) for lse in lses]
    ob = sum(e * o for e, o in zip(es, outs)) / sum(es)

    half = lax.rem(b, 2)
    mine = lax.broadcasted_iota(jnp.int32, (SUBLANES, 1), 0) // DEC_T == half

    @pl.when(half == 0)
    def _():
        oa_ref[...] = jnp.where(mine, oa_scr[...], 0.0)
        ob_ref[...] = jnp.where(mine, ob, 0.0)

    @pl.when(half == 1)
    def _():
        oa_ref[...] = jnp.where(mine, oa_scr[...], oa_ref[...])
        ob_ref[...] = jnp.where(mine, ob, ob_ref[...])


def _step_attn(caches, hs, ht, tables):
    db = caches[0].shape[0]
    per_tile = LANES // DEC_T
    per_blk = SUBLANES // DEC_T
    n_kb = len(B_PATTERNS) * B_OUT_W
    cache_specs = [pl.BlockSpec((1,) + c.shape[1:], lambda b: (b, 0, 0)) for c in caches]
    in_specs = cache_specs + [
        pl.BlockSpec((SUBLANES, QKV_W), lambda b: (b // per_blk, 0)),
        pl.BlockSpec((2 * A_KV_W, LANES), lambda b: (C_KA // (2 * A_KV_W), b // per_tile)),
        pl.BlockSpec((2 * n_kb, LANES), lambda b: (C_KB // (2 * n_kb), b // per_tile)),
    ] + [pl.BlockSpec(t.shape, lambda b, nd=t.ndim: (0,) * nd) for t in tables]
    o_specs = [pl.BlockSpec((SUBLANES, A_OUT_W), lambda b: (b // per_blk, 0)),
               pl.BlockSpec((SUBLANES, B_OUT_W), lambda b: (b // per_blk, 0))]
    return pl.pallas_call(
        _step_kernel,
        grid=(db,),
        in_specs=in_specs,
        out_specs=tuple(cache_specs + o_specs),
        out_shape=tuple([jax.ShapeDtypeStruct(c.shape, F32) for c in caches]
                        + [jax.ShapeDtypeStruct((db * DEC_T, A_OUT_W), F32), jax.ShapeDtypeStruct((db * DEC_T, B_OUT_W), F32)]),
        scratch_shapes=[pltpu.VMEM((SUBLANES, A_OUT_W), F32)],
        compiler_params=_cparams(("arbitrary",)),
        name="step_attn",
    )(*caches, hs, ht, ht, *tables)


PREP_W = C_QB - C_PAD


def _prep_w_in_kernel(w_ref, o_ref):
    j = pl.program_id(0)
    is_q = (j < C_KA // PREP_W) | ((j >= C_QB // PREP_W) & (j < C_KB // PREP_W))
    scale = jnp.where(is_q, SCALE, 1.0)
    o_ref[...] = jnp.where(j == C_PAD // PREP_W, 0.0, w_ref[...] * scale).astype(o_ref.dtype)


def _prep_w_in(w):
    pad_blk = C_PAD // PREP_W
    return pl.pallas_call(
        _prep_w_in_kernel,
        grid=(HW // PREP_W,),
        in_specs=[pl.BlockSpec((w.shape[0], PREP_W), lambda j: (0, jnp.where(j > pad_blk, j - 1, jnp.minimum(j, pad_blk - 1))))],
        out_specs=pl.BlockSpec((w.shape[0], PREP_W), lambda j: (0, j)),
        out_shape=jax.ShapeDtypeStruct((w.shape[0], HW), BF16),
        compiler_params=_cparams(("parallel",)),
        name="prep_w_in",
    )(w)


def _to_feature_major(cache):
    db, win = cache.shape[:2]
    return jnp.transpose(cache, (0, 2, 3, 4, 1)).reshape(db, -1, win)


def _from_feature_major(ct, heads):
    n, _, win = ct.shape
    return jnp.transpose(ct.reshape(n, 2, heads, HEAD_DIM, win), (0, 4, 1, 2, 3))[None]


def _tail_layers(x, h, oa, ob, w_oa, w_ob, w_out, ln1_g, ln1_b, w_ffn_in, w_ffn_out, ln2_g, ln2_b):
    t = x.shape[0]
    tm = min(1024, t)
    mixin = _gate_proj(oa, ob, h, w_oa, w_ob, min(512, t), D_MODEL)
    h1, h1b = _mm_res_ln(mixin, w_out, x, ln1_g, ln1_b, min(512, t), D_MODEL, "out_ln1", also_bf16=True)
    u = _ffn_in(h1b, w_ffn_in, min(2048, t), 512)
    return _mm_res_ln(u, w_ffn_out, h1, ln2_g, ln2_b, tm, D_FF // 4, "ffn_out_ln2")


def kernel(x_prompt, x_sample, cache_a, cache_b1, cache_b2, cache_b3, rel_bias, w_in, a_sink, w_oa, w_ob,
           w_out, ln1_g, ln1_b, w_ffn_in, w_ffn_out, ln2_g, ln2_b):
    batch, seq, _ = x_prompt.shape
    db, dt, _ = x_sample.shape
    assert dt == DEC_T and w_in.shape[0] == DEPTH and db % (SUBLANES // DEC_T) == 0
    tp, ts = batch * seq, db * dt

    w_in_b = _prep_w_in(w_in[0])
    weights = (w_oa[0].astype(BF16), w_ob[0].astype(BF16), w_out[0].astype(BF16), ln1_g, ln1_b,
               w_ffn_in[0], w_ffn_out[0].astype(BF16), ln2_g, ln2_b)
    sink = a_sink[0].astype(F32)
    table = _dist_table(rel_bias)
    b_h0 = [A_Q_HEADS + g * B_HEADS for g in range(len(B_PATTERNS))]
    b_cols = [(C_QB + g * B_OUT_W, C_KB + g * B_OUT_W, C_VB + g * B_OUT_W) for g in range(len(B_PATTERNS))]

    xp = x_prompt.reshape(tp, D_MODEL)
    hp = _in_proj(xp, w_in_b, 1024, 2048)
    bias_a = _packed_rows(_band_bias(table, A_WINDOW - 1, 1, 0, A_Q_HEADS), A_Q_HEADS // 2, 1)
    oa = _band_attn_a(hp, bias_a, sink, batch=batch, seq=seq, nblk=4)
    others, ob = [], None
    for g, (win, dil) in sorted(enumerate(B_PATTERNS), key=lambda e: -e[1][1]):
        cq, ck, cv = b_cols[g]
        bias_g = _packed_rows(_band_bias(table, win // dil, dil, b_h0[g], B_HEADS), B_HEADS // 2, 1)
        res = _dil_attn(hp, bias_g, batch=batch, seq=seq, dil=dil, nblk={1: 16, 4: 4, 16: 1}[dil], cq=cq, ck=ck, cv=cv,
                        merge=others if dil == 1 else ())
        if dil == 1:
            ob = res
        else:
            others.append(res)
    yp = _tail_layers(xp, hp, oa, ob, *weights).reshape(batch, seq, D_MODEL)

    new_a_p = _from_feature_major(_kv_tail(hp, batch=batch, seq=seq, win=min(A_WINDOW, seq), cw=A_KV_W, ck=C_KA, cv=C_VA),
                                  A_KV_HEADS)
    new_b_p = [_from_feature_major(_kv_tail(hp, batch=batch, seq=seq, win=min(win, seq), cw=B_OUT_W,
                                            ck=b_cols[g][1], cv=b_cols[g][2]), B_HEADS)
               for g, (win, dil) in enumerate(B_PATTERNS)]

    xs = x_sample.reshape(ts, D_MODEL)
    hs = _in_proj(xs, w_in_b, ts, 2048)
    lanes = -(-ts // LANES) * LANES
    ht = jnp.pad(hs[:, :QKV_W].T, ((0, 0), (0, lanes - ts)))

    group_a = A_Q_HEADS // A_KV_HEADS
    sink_rows = jnp.broadcast_to(jnp.repeat(sink.reshape(A_KV_HEADS, group_a), SUBLANES, axis=1)[:, :, None],
                                 (A_KV_HEADS, group_a * SUBLANES, LANES))
    tables = list(_step_bias(table, cache_a.shape[2], 1, 0, A_Q_HEADS, A_KV_HEADS, True)) + [sink_rows]
    for g, ((win, dil), cache) in enumerate(zip(B_PATTERNS, (cache_b1, cache_b2, cache_b3))):
        bc, bn = _step_bias(table, cache.shape[2], dil, b_h0[g], B_HEADS, 1, False)
        tables += [bc[0], bn[0]]
    caches = [_to_feature_major(c[0]) for c in (cache_a, cache_b1, cache_b2, cache_b3)]
    new_a_t, new_b1_t, new_b2_t, new_b3_t, oa_s, ob_s = _step_attn(caches, hs, ht, tables)
    ys = _tail_layers(xs, hs, oa_s.astype(BF16), ob_s.astype(BF16), *weights).reshape(db, dt, D_MODEL)

    return (yp, ys, new_a_p, new_b_p[0], new_b_p[1], new_b_p[2],
            _from_feature_major(new_a_t, A_KV_HEADS), _from_feature_major(new_b1_t, B_HEADS),
            _from_feature_major(new_b2_t, B_HEADS), _from_feature_major(new_b3_t, B_HEADS))
```

```python
import functools
import math

import numpy as np
import jax
import jax.numpy as jnp
from jax import lax
from jax.experimental import pallas as pl
from jax.experimental.pallas import tpu as pltpu

F32 = jnp.float32
BF16 = jnp.bfloat16

D_MODEL = 2048
HEAD_DIM = 64
A_WINDOW = 128
A_Q_HEADS = 16
A_KV_HEADS = 2
B_PATTERNS = ((128, 1), (512, 4), (2048, 16))
B_HEADS = 8
NUM_BUCKETS = 32
REL_MAX_DIST = 2048
BLOCK = 128
D_FF = 5632
DEPTH = 1
ALPHA = (2 * DEPTH) ** 0.25
SCALE = HEAD_DIM ** -0.5
LN_EPS = 1e-5
NEG = -1e30
LANES = 128
SUBLANES = 8
DEC_T = 4

A_OUT_W = A_Q_HEADS * HEAD_DIM
A_KV_W = A_KV_HEADS * HEAD_DIM
B_OUT_W = B_HEADS * HEAD_DIM
C_QA, C_KA, C_VA, C_PAD, C_QB, C_KB, C_VB, C_GA, C_GB = 0, 1024, 1152, 1280, 1536, 3072, 4608, 6144, 8192
HW = 10240
QKV_W = C_GA
VMEM_LIMIT = 56 * 1024 * 1024


def _cparams(sem):
    return pltpu.CompilerParams(dimension_semantics=sem, vmem_limit_bytes=VMEM_LIMIT)


def _bucket_np(dist):
    d = np.maximum(np.asarray(dist, np.int64), 0)
    max_exact = NUM_BUCKETS // 2
    ratio = np.maximum(d, max_exact).astype(np.float32) / np.float32(max_exact)
    large = max_exact + (np.log(ratio) / np.float32(math.log(REL_MAX_DIST / max_exact))
                         * np.float32(NUM_BUCKETS - max_exact)).astype(np.int32)
    return np.where(d < max_exact, d, np.minimum(large, NUM_BUCKETS - 1)).astype(np.int32)


def _dist_table(rel_bias):
    return jnp.take(rel_bias, jnp.asarray(_bucket_np(np.arange(REL_MAX_DIST + 1))), axis=0).astype(F32)


def _bias_by_dist(table, dists, h0, nh):
    d = np.asarray(dists)
    step = int(d[1] - d[0])
    assert step != 0 and np.all(np.diff(d) == step)
    lo, hi = (int(d[0]), int(d[-1])) if step > 0 else (int(d[-1]), int(d[0]))
    rows = lax.slice(table, (lo, h0), (hi + 1, h0 + nh), (abs(step), 1))
    return (rows if step > 0 else rows[::-1]).T


def _band_bias(table, max_dist, dil, h0, nh):
    u = BLOCK - np.arange(2 * BLOCK)
    near_side = _bias_by_dist(table, np.arange(BLOCK, -1, -1) * dil, h0, nh)
    w = jnp.concatenate([near_side, jnp.full((nh, BLOCK - 1), NEG, F32)], axis=1)
    w = jnp.where(jnp.asarray((u >= 0) & (u <= max_dist))[None], w, NEG)
    x = jnp.concatenate([w, jnp.full((nh, 1), NEG, F32)], axis=1)
    band = jnp.tile(x, (1, BLOCK))[:, :BLOCK * 2 * BLOCK].reshape(nh, BLOCK, 2 * BLOCK)
    return jnp.stack([band, jnp.where(jnp.asarray(np.arange(2 * BLOCK) < BLOCK), NEG, band)])


def _packed_rows(per_head, groups, pairs):
    lead, tail = per_head.shape[:-3], per_head.shape[-1]
    t = per_head.reshape(*lead, groups, pairs, 2, BLOCK, tail)
    t = jnp.swapaxes(t, -4, -3)
    return t.reshape(*lead, groups, 2 * pairs * BLOCK, tail)


def _step_bias(table, win, dil, h0, nh, n_kv, is_a):
    neg = lambda n: jnp.full((nh, n), NEG, F32)
    rows_c = []
    if is_a:
        rev = _bias_by_dist(table, np.arange(A_WINDOW - 1, -1, -1), h0, nh)
        for t in range(DEC_T):
            rows_c.append(jnp.concatenate([neg(t + 1), rev[:, :win - t - 1]], axis=1))
    elif dil == 1:
        rev = _bias_by_dist(table, np.arange(win, 0, -1), h0, nh)
        for t in range(DEC_T):
            rows_c.append(jnp.concatenate([neg(t), rev[:, :win - t]], axis=1))
    else:
        rev = _bias_by_dist(table, np.arange(win // dil, 0, -1) * dil, h0, nh)
        for t in range(DEC_T):
            slots = [rev[:, :, None] if r == t else jnp.full((nh, win // dil, 1), NEG, F32) for r in range(dil)]
            rows_c.append(jnp.concatenate(slots, axis=2).reshape(nh, win))
    bc = jnp.stack(rows_c * 2, axis=1)

    i = np.arange(SUBLANES)[:, None]
    j = np.arange(LANES)[None, :]
    dn = i % DEC_T - j % DEC_T
    vn = (j < SUBLANES) & (i // DEC_T == j // DEC_T) & (dn >= 0)
    if not is_a:
        vn &= dn % dil == 0
    near = _bias_by_dist(table, np.arange(DEC_T), h0, nh)
    bn = jnp.full((nh, SUBLANES, LANES), NEG, F32)
    for d in range(DEC_T):
        bn = jnp.where(jnp.asarray(vn & (dn == d))[None], near[:, d][:, None, None], bn)
    g8 = (nh // n_kv) * SUBLANES
    return bc.reshape(n_kv, g8, win), bn.reshape(n_kv, g8, LANES)


def _inproj_kernel(x_ref, w_ref, o_ref, xb_ref):
    @pl.when(pl.program_id(1) == 0)
    def _():
        xb_ref[...] = x_ref[...].astype(BF16)

    o_ref[...] = jnp.dot(xb_ref[...], w_ref[...], preferred_element_type=F32)


def _in_proj(x, w, tm, tn):
    t, k = x.shape
    n = w.shape[1]
    return pl.pallas_call(
        _inproj_kernel,
        grid=(t // tm, n // tn),
        in_specs=[pl.BlockSpec((tm, k), lambda i, j: (i, 0)),
                  pl.BlockSpec((k, tn), lambda i, j: (0, j))],
        out_specs=pl.BlockSpec((tm, tn), lambda i, j: (i, j)),
        out_shape=jax.ShapeDtypeStruct((t, n), F32),
        scratch_shapes=[pltpu.VMEM((tm, k), BF16)],
        compiler_params=_cparams(("parallel", "arbitrary")),
        name="in_proj",
    )(x, w)


def _gate_proj_kernel(oa_ref, ob_ref, ga_ref, gb_ref, woa_ref, wob_ref, o_ref):
    pa = jnp.dot(oa_ref[...], woa_ref[...], preferred_element_type=F32)
    pb = jnp.dot(ob_ref[...], wob_ref[...], preferred_element_type=F32)
    sa = 0.5 * jnp.tanh(0.5 * ga_ref[...]) + 0.5
    sb = 0.5 * jnp.tanh(0.5 * gb_ref[...]) + 0.5
    o_ref[...] = (sa * pa + sb * pb).astype(o_ref.dtype)


def _gate_proj(oa, ob, h, w_oa, w_ob, tm, tn):
    t = oa.shape[0]
    nj = D_MODEL // tn
    return pl.pallas_call(
        _gate_proj_kernel,
        grid=(t // tm, nj),
        in_specs=[pl.BlockSpec((tm, A_OUT_W), lambda i, j: (i, 0)),
                  pl.BlockSpec((tm, B_OUT_W), lambda i, j: (i, 0)),
                  pl.BlockSpec((tm, tn), lambda i, j: (i, C_GA // tn + j)),
                  pl.BlockSpec((tm, tn), lambda i, j: (i, C_GB // tn + j)),
                  pl.BlockSpec((A_OUT_W, tn), lambda i, j: (0, j)),
                  pl.BlockSpec((B_OUT_W, tn), lambda i, j: (0, j))],
        out_specs=pl.BlockSpec((tm, tn), lambda i, j: (i, j)),
        out_shape=jax.ShapeDtypeStruct((t, D_MODEL), BF16),
        compiler_params=_cparams(("parallel", "arbitrary")),
        name="gate_proj",
    )(oa, ob, h, h, w_oa, w_ob)


def _layer_norm(z, g, b):
    mu = jnp.mean(z, axis=-1, keepdims=True)
    zc = z - mu
    var = jnp.mean(zc * zc, axis=-1, keepdims=True)
    return zc * lax.rsqrt(var + LN_EPS) * g + b


def _mm_res_ln_kernel(a_ref, w_ref, r_ref, g_ref, b_ref, o_ref, ob_ref=None, *, n_chunk):
    k = pl.program_id(1)
    nk = pl.num_programs(1)
    if n_chunk is None:
        sub = min(128, o_ref.shape[0])
        for r0 in range(0, o_ref.shape[0], sub):
            rows = slice(r0, r0 + sub)
            mix = jnp.dot(a_ref[rows, :], w_ref[...], preferred_element_type=F32)
            y = _layer_norm(ALPHA * r_ref[rows, :] + mix, g_ref[...], b_ref[...])
            o_ref[rows, :] = y
            if ob_ref is not None:
                ob_ref[rows, :] = y.astype(ob_ref.dtype)
        return

    def accumulate(first):
        a = a_ref[...]
        for c in range(D_MODEL // n_chunk):
            cols = slice(c * n_chunk, (c + 1) * n_chunk)
            part = jnp.dot(a, w_ref[:, cols], preferred_element_type=F32)
            o_ref[:, cols] = part if first else o_ref[:, cols] + part

    pl.when(k == 0)(lambda: accumulate(True))
    pl.when(k > 0)(lambda: accumulate(False))

    @pl.when(k == nk - 1)
    def _():
        rows_per_pass = 256
        for r0 in range(0, o_ref.shape[0], rows_per_pass):
            rows = slice(r0, r0 + rows_per_pass)
            o_ref[rows, :] = _layer_norm(ALPHA * r_ref[rows, :] + o_ref[rows, :], g_ref[...], b_ref[...])


def _mm_res_ln(a, w, res, g, b, tm, tk, name, also_bf16=False):
    t, kdim = a.shape
    assert not also_bf16 or tk == kdim
    o_spec = pl.BlockSpec((tm, D_MODEL), lambda i, k: (i, 0))
    o_shape = jax.ShapeDtypeStruct((t, D_MODEL), F32)
    return pl.pallas_call(
        functools.partial(_mm_res_ln_kernel, n_chunk=None if tk == kdim else 512),
        grid=(t // tm, kdim // tk),
        in_specs=[pl.BlockSpec((tm, tk), lambda i, k: (i, k)),
                  pl.BlockSpec((tk, D_MODEL), lambda i, k: (k, 0)),
                  pl.BlockSpec((tm, D_MODEL), lambda i, k: (i, 0)),
                  pl.BlockSpec((1, D_MODEL), lambda i, k: (0, 0)),
                  pl.BlockSpec((1, D_MODEL), lambda i, k: (0, 0))],
        out_specs=(o_spec, o_spec) if also_bf16 else o_spec,
        out_shape=(o_shape, jax.ShapeDtypeStruct((t, D_MODEL), BF16)) if also_bf16 else o_shape,
        compiler_params=_cparams(("parallel", "arbitrary")),
        name=name,
    )(a, w, res, g, b)


def _ffn_in_kernel(h_ref, wg_ref, wu_ref, o_ref, *, sub):
    wg = wg_ref[...].astype(BF16)
    wu = wu_ref[...].astype(BF16)
    for r0 in range(0, o_ref.shape[0], sub):
        rows = slice(r0, r0 + sub)
        hb = h_ref[rows, :]
        gate = jnp.dot(hb, wg, preferred_element_type=F32)
        up = jnp.dot(hb, wu, preferred_element_type=F32)
        o_ref[rows, :] = (gate * jax.nn.sigmoid(gate) * up).astype(o_ref.dtype)


def _ffn_in(h1b, w_ffn_in, tm, tn):
    t = h1b.shape[0]
    nj = D_FF // tn
    return pl.pallas_call(
        functools.partial(_ffn_in_kernel, sub=min(1024, tm)),
        grid=(t // tm, nj),
        in_specs=[pl.BlockSpec((tm, D_MODEL), lambda i, j: (i, 0)),
                  pl.BlockSpec((D_MODEL, tn), lambda i, j: (0, j)),
                  pl.BlockSpec((D_MODEL, tn), lambda i, j: (0, nj + j))],
        out_specs=pl.BlockSpec((tm, tn), lambda i, j: (i, j)),
        out_shape=jax.ShapeDtypeStruct((t, D_FF), BF16),
        compiler_params=_cparams(("parallel", "arbitrary")),
        name="ffn_in",
    )(h1b, w_ffn_in, w_ffn_in)


def _head_lane_masks():
    lane = lax.broadcasted_iota(jnp.int32, (1, LANES), 1)
    return [lane < HEAD_DIM, lane >= HEAD_DIM]


def _packed_attention(q, k2b, v2, bias, masks, sinks):
    q2 = jnp.concatenate([jnp.where(mk, q, 0.0) for mk in masks], axis=0).astype(BF16)
    s = lax.dot_general(q2, k2b, (((1,), (1,)), ((), ())), preferred_element_type=F32) + bias
    m = jnp.max(s, axis=-1, keepdims=True)
    if sinks is not None:
        sink = jnp.where(lax.broadcasted_iota(jnp.int32, (2 * BLOCK, 1), 0) < BLOCK, sinks[0], sinks[1])
        m = jnp.maximum(m, sink)
    p = jnp.exp(s - m).astype(BF16)
    nd = jnp.zeros((BLOCK, 2 * LANES), F32)
    for hh, mk in enumerate(masks):
        ones = jnp.broadcast_to(jnp.where(mk, 1.0, 0.0), v2.shape)
        w = jnp.concatenate([jnp.where(mk, v2, 0.0), ones], axis=1).astype(BF16)
        nd = nd + jnp.dot(p[hh * BLOCK:(hh + 1) * BLOCK], w, preferred_element_type=F32)
    den = nd[:, LANES:]
    if sinks is not None:
        es = jnp.exp(sink - m)
        den = den + jnp.where(masks[0], es[:BLOCK], es[BLOCK:])
    return nd[:, :LANES] / den, jnp.where(masks[0], m[:BLOCK], m[BLOCK:]) + jnp.log(den)


def _band_attn_kernel(q_ref, kc_ref, kp_ref, vc_ref, vp_ref, bias_ref, sink_ref, o_ref):
    first = (pl.program_id(1) == 0).astype(jnp.int32)
    masks = _head_lane_masks()
    pairs_per_kv = (A_Q_HEADS // A_KV_HEADS) // 2
    for j in range(q_ref.shape[0] // BLOCK):
        rows = slice(j * BLOCK, (j + 1) * BLOCK)
        if j == 0:
            kp, vp, dead = kp_ref[...], vp_ref[...], first
        else:
            kp, vp, dead = kc_ref[(j - 1) * BLOCK:j * BLOCK, :], vc_ref[(j - 1) * BLOCK:j * BLOCK, :], 0
        kx = jnp.concatenate([kp, kc_ref[rows, :]], axis=0)
        vx = jnp.concatenate([vp, vc_ref[rows, :]], axis=0)
        kr = pltpu.roll(kx, HEAD_DIM, 1)
        vr = pltpu.roll(vx, HEAD_DIM, 1)
        for kv in range(A_KV_HEADS):
            k2 = jnp.where(masks[kv], kx, kr).astype(BF16)
            v2 = jnp.where(masks[kv], vx, vr)
            for pp in range(pairs_per_kv):
                pi = kv * pairs_per_kv + pp
                sl = slice(pi * LANES, (pi + 1) * LANES)
                out, _ = _packed_attention(q_ref[rows, sl], k2, v2, bias_ref[dead, pi], masks,
                                           (sink_ref[2 * pi], sink_ref[2 * pi + 1]))
                o_ref[rows, sl] = out.astype(o_ref.dtype)


def _band_attn_a(h, bias, sink, *, batch, seq, nblk):
    chunk = BLOCK * nblk
    nc = seq // chunk
    qw, kw = A_OUT_W, A_KV_W

    def cur(b, i):
        return b * nc + i

    def prev(b, i):
        return jnp.maximum((b * nc + i) * nblk - 1, 0)

    return pl.pallas_call(
        _band_attn_kernel,
        grid=(batch, nc),
        in_specs=[pl.BlockSpec((chunk, qw), lambda b, i: (cur(b, i), C_QA // qw)),
                  pl.BlockSpec((chunk, kw), lambda b, i: (cur(b, i), C_KA // kw)),
                  pl.BlockSpec((BLOCK, kw), lambda b, i: (prev(b, i), C_KA // kw)),
                  pl.BlockSpec((chunk, kw), lambda b, i: (cur(b, i), C_VA // kw)),
                  pl.BlockSpec((BLOCK, kw), lambda b, i: (prev(b, i), C_VA // kw)),
                  pl.BlockSpec(bias.shape, lambda b, i: (0, 0, 0, 0)),
                  pl.BlockSpec(memory_space=pltpu.SMEM)],
        out_specs=pl.BlockSpec((chunk, qw), lambda b, i: (cur(b, i), 0)),
        out_shape=jax.ShapeDtypeStruct((batch * seq, qw), BF16),
        compiler_params=_cparams(("parallel", "arbitrary")),
        name="band_attn_a",
    )(h, h, h, h, h, bias, sink)


def _dil_attn_kernel(*refs, dil, nblk, unroll, has_prev, n_merge, emit_kvt):
    refs = list(refs)
    q_ref, kc_ref, vc_ref = refs[:3]
    del refs[:3]
    if has_prev:
        kp_ref, vp_ref = refs[:2]
        del refs[:2]
    bias_ref = refs.pop(0)
    others = []
    if n_merge:
        others = [(refs[2 * g], refs[2 * g + 1]) for g in range(n_merge)]
        del refs[:2 * n_merge]
        (o_ref,) = refs
    elif emit_kvt:
        o_ref, lse_ref, kvt_ref = refs
        kvt_ref[0, 0] = kc_ref[...].T
        kvt_ref[0, 1] = vc_ref[...].T
    else:
        o_ref, lse_ref = refs
    first = (pl.program_id(2) == 0).astype(jnp.int32)
    masks = _head_lane_masks()

    def rows(r, j):
        return pl.ds(j * BLOCK * dil + r, BLOCK, stride=dil) if dil > 1 else pl.ds(j * BLOCK, BLOCK)

    def tile(r, j):
        cur = rows(r, j)
        if not has_prev:
            k2, v2, bias = kc_ref[cur, :].astype(BF16), vc_ref[cur, :], bias_ref[0, 0][:, BLOCK:]
        else:
            if j == 0:
                kp, vp, dead = kp_ref[rows(r, 0), :], vp_ref[rows(r, 0), :], first
            else:
                kp, vp, dead = kc_ref[rows(r, j - 1), :], vc_ref[rows(r, j - 1), :], 0
            k2 = jnp.concatenate([kp, kc_ref[cur, :]], axis=0).astype(BF16)
            v2 = jnp.concatenate([vp, vc_ref[cur, :]], axis=0)
            bias = bias_ref[dead, 0]
        out, lse = _packed_attention(q_ref[cur, :], k2, v2, bias, masks, None)
        if not n_merge:
            o_ref[cur, :] = out
            lse_ref[cur, :] = lse
            return
        lses = [lse] + [l_ref[cur, :] for _, l_ref in others]
        outs = [out] + [og_ref[cur, :] for og_ref, _ in others]
        mx = functools.reduce(jnp.maximum, lses)
        es = [jnp.exp(l - mx) for l in lses]
        o_ref[cur, :] = (sum(e * o for e, o in zip(es, outs)) / sum(es)).astype(o_ref.dtype)

    def body(it, carry):
        for u in range(unroll):
            for j in range(nblk):
                tile(it * unroll + u, j)
        return carry

    if dil == unroll:
        body(0, 0)
    else:
        lax.fori_loop(0, dil // unroll, body, 0)


def _dil_attn(h, bias, *, batch, seq, dil, nblk, cq, ck, cv, merge=()):
    band = BLOCK * dil
    chunk = band * nblk
    nc = seq // chunk
    pairs = B_OUT_W // LANES

    def cur(b, p, i):
        return b * nc + i

    def prev(b, p, i):
        return jnp.maximum((b * nc + i) * nblk - 1, 0)

    o_spec = pl.BlockSpec((chunk, LANES), lambda b, p, i: (cur(b, p, i), p))
    o_shape = jax.ShapeDtypeStruct((batch * seq, B_OUT_W), F32)
    has_prev = seq > band
    in_specs = [pl.BlockSpec((chunk, LANES), lambda b, p, i: (cur(b, p, i), cq // LANES + p)),
                pl.BlockSpec((chunk, LANES), lambda b, p, i: (cur(b, p, i), ck // LANES + p)),
                pl.BlockSpec((chunk, LANES), lambda b, p, i: (cur(b, p, i), cv // LANES + p))]
    if has_prev:
        in_specs += [pl.BlockSpec((band, LANES), lambda b, p, i: (prev(b, p, i), ck // LANES + p)),
                     pl.BlockSpec((band, LANES), lambda b, p, i: (prev(b, p, i), cv // LANES + p))]
    n_h = len(in_specs)
    in_specs.append(pl.BlockSpec((2, 1, 2 * BLOCK, 2 * BLOCK), lambda b, p, i: (0, p, 0, 0)))
    others = [a for pair in merge for a in pair]
    in_specs += [o_spec] * len(others)
    emit_kvt = False
    if merge:
        assert dil == 1, "the bf16 output needs unstrided stores"
        out_specs, out_shape = o_spec, jax.ShapeDtypeStruct((batch * seq, B_OUT_W), BF16)
    elif chunk == seq:
        emit_kvt = True
        out_specs = (o_spec, o_spec, pl.BlockSpec((1, 2, LANES, seq), lambda b, p, i: (b, 0, p, 0)))
        out_shape = (o_shape, o_shape, jax.ShapeDtypeStruct((batch, 2, B_OUT_W, seq), F32))
    else:
        out_specs, out_shape = (o_spec, o_spec), (o_shape, o_shape)
    return pl.pallas_call(
        functools.partial(_dil_attn_kernel, dil=dil, nblk=nblk, unroll=max(1, min(dil, 8 // nblk)), has_prev=has_prev,
                          n_merge=len(merge), emit_kvt=emit_kvt),
        grid=(batch, pairs, nc),
        in_specs=in_specs,
        out_specs=out_specs,
        out_shape=out_shape,
        compiler_params=_cparams(("parallel", "parallel", "arbitrary")),
        name=f"dil_attn_d{dil}",
    )(*([h] * n_h), bias, *others)


def _kv_tail_kernel(k_ref, v_ref, o_ref):
    cw = k_ref.shape[1]
    o_ref[0, 0:cw, :] = k_ref[...].T
    o_ref[0, cw:2 * cw, :] = v_ref[...].T


def _kv_tail(h, *, batch, seq, win, cw, ck, cv):
    rows = min(win, 8 * BLOCK)
    nblk = win // rows
    base = (seq - win) // rows
    per = seq // rows
    return pl.pallas_call(
        _kv_tail_kernel,
        grid=(batch, nblk),
        in_specs=[pl.BlockSpec((rows, cw), lambda b, i: (b * per + base + i, ck // cw)),
                  pl.BlockSpec((rows, cw), lambda b, i: (b * per + base + i, cv // cw))],
        out_specs=pl.BlockSpec((1, 2 * cw, rows), lambda b, i: (b, 0, i)),
        out_shape=jax.ShapeDtypeStruct((batch, 2 * cw, win), F32),
        compiler_params=_cparams(("parallel", "parallel")),
        name=f"kv_tail_w{win}_c{cw}",
    )(h, h)


def _shift_window(c_ref, cout_ref, knf, vnf, lo):
    lane = lax.broadcasted_iota(jnp.int32, (1, LANES), 1)
    keep = lane < LANES - DEC_T
    to_tail = lax.rem(2 * LANES - DEC_T - lo, LANES)
    new_tail = jnp.concatenate([pltpu.roll(knf, to_tail, 1), pltpu.roll(vnf, to_tail, 1)], axis=0)
    nlb = c_ref.shape[2] // LANES
    nxt = pltpu.roll(c_ref[0, :, 0:LANES], LANES - DEC_T, 1)
    for j in range(nlb):
        cur = nxt
        if j + 1 < nlb:
            nxt = pltpu.roll(c_ref[0, :, (j + 1) * LANES:(j + 2) * LANES], LANES - DEC_T, 1)
        else:
            nxt = new_tail
        cout_ref[0, :, j * LANES:(j + 1) * LANES] = jnp.where(keep, cur, nxt)


def _pad_rows(x):
    return jnp.concatenate([x, jnp.zeros((LANES - SUBLANES, x.shape[1]), F32)], axis=0).astype(BF16)


def _step_attend_a(c_ref, q, knt, vnt, bc_ref, bn_ref, sink_ref, o_scr):
    kvw = A_KV_W
    group = A_Q_HEADS // A_KV_HEADS
    for kv in range(A_KV_HEADS):
        ksl = slice(kv * HEAD_DIM, (kv + 1) * HEAD_DIM)
        kt = c_ref[0, ksl, :].astype(BF16)
        vt = c_ref[0, kvw + kv * HEAD_DIM:kvw + (kv + 1) * HEAD_DIM, :].astype(BF16)
        heads = range(kv * group, (kv + 1) * group)
        qs = jnp.concatenate([q[:, h * HEAD_DIM:(h + 1) * HEAD_DIM] for h in heads], axis=0).astype(BF16)
        s = jnp.dot(qs, kt, preferred_element_type=F32) + bc_ref[kv]
        sn = lax.dot_general(qs, _pad_rows(knt[:, ksl]), (((1,), (1,)), ((), ())), preferred_element_type=F32) + bn_ref[kv]
        sink = sink_ref[kv][:, 0:1]
        m = jnp.maximum(jnp.maximum(jnp.max(s, axis=-1, keepdims=True), jnp.max(sn, axis=-1, keepdims=True)), sink)
        p = jnp.exp(s - m)
        pn = jnp.exp(sn - m)
        l = jnp.sum(p, axis=-1, keepdims=True) + jnp.sum(pn, axis=-1, keepdims=True) + jnp.exp(sink - m)
        o = lax.dot_general(p.astype(BF16), vt, (((1,), (1,)), ((), ())), preferred_element_type=F32)
        o = (o + jnp.dot(pn.astype(BF16), _pad_rows(vnt[:, ksl]), preferred_element_type=F32)) / l
        for g, h in enumerate(heads):
            o_scr[:, h * HEAD_DIM:(h + 1) * HEAD_DIM] = o[g * SUBLANES:(g + 1) * SUBLANES, :]


def _step_attend_b(c_ref, q, knt, vnt, bc_ref, bn_ref, head_lanes):
    kw = B_OUT_W
    qx = jnp.concatenate([jnp.where(mk, q, 0.0) for mk in head_lanes], axis=0).astype(BF16)
    kt = c_ref[0, 0:kw, :].astype(BF16)
    vt = c_ref[0, kw:2 * kw, :].astype(BF16)
    s = jnp.dot(qx, kt, preferred_element_type=F32) + bc_ref[...]
    sn = lax.dot_general(qx, _pad_rows(knt), (((1,), (1,)), ((), ())), preferred_element_type=F32) + bn_ref[...]
    m = jnp.maximum(jnp.max(s, axis=-1, keepdims=True), jnp.max(sn, axis=-1, keepdims=True))
    p = jnp.exp(s - m)
    pn = jnp.exp(sn - m)
    l = jnp.sum(p, axis=-1, keepdims=True) + jnp.sum(pn, axis=-1, keepdims=True)
    ox = lax.dot_general(p.astype(BF16), vt, (((1,), (1,)), ((), ())), preferred_element_type=F32)
    ox = ox + jnp.dot(pn.astype(BF16), _pad_rows(vnt), preferred_element_type=F32)
    o = jnp.zeros((SUBLANES, kw), F32)
    lrow = jnp.ones((SUBLANES, kw), F32)
    mrow = jnp.zeros((SUBLANES, kw), F32)
    for h, mk in enumerate(head_lanes):
        rows = slice(h * SUBLANES, (h + 1) * SUBLANES)
        o = jnp.where(mk, ox[rows], o)
        lrow = jnp.where(mk, l[rows], lrow)
        mrow = jnp.where(mk, m[rows], mrow)
    return o / lrow, mrow + jnp.log(lrow)


def _step_kernel(ca_ref, c1_ref, c2_ref, c3_ref, hq_ref, fa_ref, fb_ref,
                 bca_ref, bna_ref, sink_ref, bc1_ref, bn1_ref, bc2_ref, bn2_ref, bc3_ref, bn3_ref,
                 na_ref, n1_ref, n2_ref, n3_ref, oa_ref, ob_ref, oa_scr):
    b = pl.program_id(0)
    lo = DEC_T * lax.rem(b, LANES // DEC_T)

    _shift_window(ca_ref, na_ref, fa_ref[0:A_KV_W, :], fa_ref[A_KV_W:2 * A_KV_W, :], lo)
    groups = ((c1_ref, n1_ref, bc1_ref, bn1_ref), (c2_ref, n2_ref, bc2_ref, bn2_ref), (c3_ref, n3_ref, bc3_ref, bn3_ref))
    n_groups = len(groups)
    for g, (c_ref, n_ref, _, _) in enumerate(groups):
        _shift_window(c_ref, n_ref, fb_ref[g * B_OUT_W:(g + 1) * B_OUT_W, :],
                      fb_ref[(n_groups + g) * B_OUT_W:(n_groups + g + 1) * B_OUT_W, :], lo)

    _step_attend_a(ca_ref, hq_ref[:, C_QA:C_QA + A_OUT_W], hq_ref[:, C_KA:C_KA + A_KV_W], hq_ref[:, C_VA:C_VA + A_KV_W],
                   bca_ref, bna_ref, sink_ref, oa_scr)

    lane = lax.broadcasted_iota(jnp.int32, (1, B_OUT_W), 1)
    head_lanes = [lane // HEAD_DIM == h for h in range(B_HEADS)]
    outs, lses = [], []
    for g, (c_ref, _, bc_ref, bn_ref) in enumerate(groups):
        sl = lambda c0: slice(c0 + g * B_OUT_W, c0 + (g + 1) * B_OUT_W)
        o, lse = _step_attend_b(c_ref, hq_ref[:, sl(C_QB)], hq_ref[:, sl(C_KB)], hq_ref[:, sl(C_VB)], bc_ref, bn_ref,
                                head_lanes)
        outs.append(o)
        lses.append(lse)
    mx = functools.reduce(jnp.maximum, lses)
    es = [jnp.exp(lse - mx) for lse in lses]
    ob = sum(e * o for e, o in zip(es, outs)) / sum(es)

    half = lax.rem(b, 2)
    mine = lax.broadcasted_iota(jnp.int32, (SUBLANES, 1), 0) // DEC_T == half

    @pl.when(half == 0)
    def _():
        oa_ref[...] = jnp.where(mine, oa_scr[...], 0.0)
        ob_ref[...] = jnp.where(mine, ob, 0.0)

    @pl.when(half == 1)
    def _():
        oa_ref[...] = jnp.where(mine, oa_scr[...], oa_ref[...])
        ob_ref[...] = jnp.where(mine, ob, ob_ref[...])


def _step_attn(caches, hs, ht, tables):
    db = caches[0].shape[0]
    per_tile = LANES // DEC_T
    per_blk = SUBLANES // DEC_T
    n_kb = len(B_PATTERNS) * B_OUT_W
    cache_specs = [pl.BlockSpec((1,) + c.shape[1:], lambda b: (b, 0, 0)) for c in caches]
    in_specs = cache_specs + [
        pl.BlockSpec((SUBLANES, QKV_W), lambda b: (b // per_blk, 0)),
        pl.BlockSpec((2 * A_KV_W, LANES), lambda b: (C_KA // (2 * A_KV_W), b // per_tile)),
        pl.BlockSpec((2 * n_kb, LANES), lambda b: (C_KB // (2 * n_kb), b // per_tile)),
    ] + [pl.BlockSpec(t.shape, lambda b, nd=t.ndim: (0,) * nd) for t in tables]
    o_specs = [pl.BlockSpec((SUBLANES, A_OUT_W), lambda b: (b // per_blk, 0)),
               pl.BlockSpec((SUBLANES, B_OUT_W), lambda b: (b // per_blk, 0))]
    return pl.pallas_call(
        _step_kernel,
        grid=(db,),
        in_specs=in_specs,
        out_specs=tuple(cache_specs + o_specs),
        out_shape=tuple([jax.ShapeDtypeStruct(c.shape, F32) for c in caches]
                        + [jax.ShapeDtypeStruct((db * DEC_T, A_OUT_W), F32), jax.ShapeDtypeStruct((db * DEC_T, B_OUT_W), F32)]),
        scratch_shapes=[pltpu.VMEM((SUBLANES, A_OUT_W), F32)],
        compiler_params=_cparams(("arbitrary",)),
        name="step_attn",
    )(*caches, hs, ht, ht, *tables)


PREP_W = C_QB - C_PAD


def _prep_w_in_kernel(w_ref, o_ref):
    j = pl.program_id(0)
    is_q = (j < C_KA // PREP_W) | ((j >= C_QB // PREP_W) & (j < C_KB // PREP_W))
    scale = jnp.where(is_q, SCALE, 1.0)
    o_ref[...] = jnp.where(j == C_PAD // PREP_W, 0.0, w_ref[...] * scale).astype(o_ref.dtype)


def _prep_w_in(w):
    pad_blk = C_PAD // PREP_W
    return pl.pallas_call(
        _prep_w_in_kernel,
        grid=(HW // PREP_W,),
        in_specs=[pl.BlockSpec((w.shape[0], PREP_W), lambda j: (0, jnp.where(j > pad_blk, j - 1, jnp.minimum(j, pad_blk - 1))))],
        out_specs=pl.BlockSpec((w.shape[0], PREP_W), lambda j: (0, j)),
        out_shape=jax.ShapeDtypeStruct((w.shape[0], HW), BF16),
        compiler_params=_cparams(("parallel",)),
        name="prep_w_in",
    )(w)


def _to_feature_major(cache):
    db, win = cache.shape[:2]
    return jnp.transpose(cache, (0, 2, 3, 4, 1)).reshape(db, -1, win)


def _from_feature_major(ct, heads):
    n, _, win = ct.shape
    return jnp.transpose(ct.reshape(n, 2, heads, HEAD_DIM, win), (0, 4, 1, 2, 3))[None]


def _tail_layers(x, h, oa, ob, w_oa, w_ob, w_out, ln1_g, ln1_b, w_ffn_in, w_ffn_out, ln2_g, ln2_b):
    t = x.shape[0]
    tm = min(1024, t)
    mixin = _gate_proj(oa, ob, h, w_oa, w_ob, min(512, t), D_MODEL)
    h1, h1b = _mm_res_ln(mixin, w_out, x, ln1_g, ln1_b, min(512, t), D_MODEL, "out_ln1", also_bf16=True)
    u = _ffn_in(h1b, w_ffn_in, min(2048, t), 512)
    return _mm_res_ln(u, w_ffn_out, h1, ln2_g, ln2_b, tm, D_FF // 4, "ffn_out_ln2")


def kernel(x_prompt, x_sample, cache_a, cache_b1, cache_b2, cache_b3, rel_bias, w_in, a_sink, w_oa, w_ob,
           w_out, ln1_g, ln1_b, w_ffn_in, w_ffn_out, ln2_g, ln2_b):
    batch, seq, _ = x_prompt.shape
    db, dt, _ = x_sample.shape
    assert dt == DEC_T and w_in.shape[0] == DEPTH and db % (SUBLANES // DEC_T) == 0
    tp, ts = batch * seq, db * dt

    w_in_b = _prep_w_in(w_in[0])
    weights = (w_oa[0].astype(BF16), w_ob[0].astype(BF16), w_out[0].astype(BF16), ln1_g, ln1_b,
               w_ffn_in[0], w_ffn_out[0].astype(BF16), ln2_g, ln2_b)
    sink = a_sink[0].astype(F32)
    table = _dist_table(rel_bias)
    b_h0 = [A_Q_HEADS + g * B_HEADS for g in range(len(B_PATTERNS))]
    b_cols = [(C_QB + g * B_OUT_W, C_KB + g * B_OUT_W, C_VB + g * B_OUT_W) for g in range(len(B_PATTERNS))]

    xp = x_prompt.reshape(tp, D_MODEL)
    hp = _in_proj(xp, w_in_b, 1024, 2048)
    bias_a = _packed_rows(_band_bias(table, A_WINDOW - 1, 1, 0, A_Q_HEADS), A_Q_HEADS // 2, 1)
    oa = _band_attn_a(hp, bias_a, sink, batch=batch, seq=seq, nblk=4)
    others, ob, kvt = [], None, {}
    for g, (win, dil) in sorted(enumerate(B_PATTERNS), key=lambda e: -e[1][1]):
        cq, ck, cv = b_cols[g]
        bias_g = _packed_rows(_band_bias(table, win // dil, dil, b_h0[g], B_HEADS), B_HEADS // 2, 1)
        res = _dil_attn(hp, bias_g, batch=batch, seq=seq, dil=dil, nblk={1: 16, 4: 4, 16: 1}[dil], cq=cq, ck=ck, cv=cv,
                        merge=others if dil == 1 else ())
        if dil == 1:
            ob = res
        else:
            others.append(res[:2])
            if len(res) == 3 and win >= seq:
                kvt[g] = res[2].reshape(batch, 2 * B_OUT_W, seq)
    yp = _tail_layers(xp, hp, oa, ob, *weights).reshape(batch, seq, D_MODEL)

    new_a_p = _from_feature_major(_kv_tail(hp, batch=batch, seq=seq, win=min(A_WINDOW, seq), cw=A_KV_W, ck=C_KA, cv=C_VA),
                                  A_KV_HEADS)
    new_b_p = [_from_feature_major(kvt[g] if g in kvt else _kv_tail(hp, batch=batch, seq=seq, win=min(win, seq), cw=B_OUT_W,
                                                                    ck=b_cols[g][1], cv=b_cols[g][2]), B_HEADS)
               for g, (win, dil) in enumerate(B_PATTERNS)]

    xs = x_sample.reshape(ts, D_MODEL)
    hs = _in_proj(xs, w_in_b, ts, 2048)
    lanes = -(-ts // LANES) * LANES
    ht = jnp.pad(hs[:, :QKV_W].T, ((0, 0), (0, lanes - ts)))

    group_a = A_Q_HEADS // A_KV_HEADS
    sink_rows = jnp.broadcast_to(jnp.repeat(sink.reshape(A_KV_HEADS, group_a), SUBLANES, axis=1)[:, :, None],
                                 (A_KV_HEADS, group_a * SUBLANES, LANES))
    tables = list(_step_bias(table, cache_a.shape[2], 1, 0, A_Q_HEADS, A_KV_HEADS, True)) + [sink_rows]
    for g, ((win, dil), cache) in enumerate(zip(B_PATTERNS, (cache_b1, cache_b2, cache_b3))):
        bc, bn = _step_bias(table, cache.shape[2], dil, b_h0[g], B_HEADS, 1, False)
        tables += [bc[0], bn[0]]
    caches = [_to_feature_major(c[0]) for c in (cache_a, cache_b1, cache_b2, cache_b3)]
    new_a_t, new_b1_t, new_b2_t, new_b3_t, oa_s, ob_s = _step_attn(caches, hs, ht, tables)
    ys = _tail_layers(xs, hs, oa_s.astype(BF16), ob_s.astype(BF16), *weights).reshape(db, dt, D_MODEL)

    return (yp, ys, new_a_p, new_b_p[0], new_b_p[1], new_b_p[2],
            _from_feature_major(new_a_t, A_KV_HEADS), _from_feature_major(new_b1_t, B_HEADS),
            _from_feature_major(new_b2_t, B_HEADS), _from_feature_major(new_b3_t, B_HEADS))
```
